```python
import math
import jax, jax.numpy as jnp
from jax import lax
import numpy as np

D_MODEL = 4096
BATCH = 4
SEQ = 4096
DEPTH = 2

MEM_LEN = 256
EPS = 1e-5
HG_DK = 128
HG_HEADS = D_MODEL // HG_DK
HG_DV = 128
HG_WIDTH = HG_HEADS * HG_DK
HG_CHUNK = 64
HG_SUB = 16
SSD_HEADDIM = 64
SSD_INNER = D_MODEL
SSD_HEADS = SSD_INNER // SSD_HEADDIM
SSD_GROUPS = 8
SSD_STATE = 128
SSD_CONV = 4
SSD_CHUNK = 128
SSD_CONV_CH = SSD_INNER + 2 * SSD_GROUPS * SSD_STATE
SSD_HPG = SSD_HEADS // SSD_GROUPS
XA_HEADS = 4
XA_HEAD_DIM = D_MODEL // XA_HEADS
D_FF = 4 * D_MODEL
N_BRANCH = 2
IN_COLS = 4 * HG_WIDTH + SSD_INNER + SSD_CONV_CH + SSD_HEADS + N_BRANCH * D_MODEL

kernel_name = "hgrn2_ssd_gated_hybrid_memxattn"


def rms_norm(x, w):
    xf = x.astype(jnp.float32)
    y = xf * lax.rsqrt(jnp.mean(xf * xf, axis=-1, keepdims=True) + EPS)
    return (y * w.astype(jnp.float32)).astype(x.dtype)


def _gla_chunked(q, k, v, log_f):
    nc, bsz, h, c, dk = q.shape
    dv = v.shape[-1]
    ns, L = c // HG_SUB, HG_SUB
    pos = jnp.arange(c)
    off_mask = pos[None, :] < (jnp.arange(ns) * L)[:, None]
    tri = jnp.tril(jnp.ones((L, L), bool))
    eye_ns = jnp.eye(ns, dtype=jnp.float32)

    def step(S, inp):
        qc, kc, vc, gc = inp
        b = jnp.cumsum(gc, axis=2)
        bs = b.reshape(bsz, h, ns, L, dk)
        qs = qc.reshape(bsz, h, ns, L, dk)
        ks = kc.reshape(bsz, h, ns, L, dk)
        b_ref = jnp.concatenate([jnp.zeros_like(bs[:, :, :1, 0]), bs[:, :, :-1, -1]], axis=2)
        q_off = qs * jnp.exp(bs - b_ref[:, :, :, None])
        k_exp = jnp.where(off_mask[None, None, :, :, None], b_ref[:, :, :, None] - b[:, :, None], -jnp.inf)
        k_off = kc[:, :, None] * jnp.exp(k_exp)
        a_off = jnp.einsum('bhntd,bhnsd->bhnts', q_off, k_off)
        d_exp = jnp.where(tri[:, :, None], bs[:, :, :, :, None] - bs[:, :, :, None], -jnp.inf)
        a_diag = jnp.einsum('bhntd,bhnsd,bhntsd->bhnts', qs, ks, jnp.exp(d_exp))
        a_diag = jnp.einsum('bhnts,nm->bhntms', a_diag, eye_ns).reshape(bsz, h, ns, L, c)
        a = (a_off + a_diag).reshape(bsz, h, c, c)
        o = jnp.einsum('bhts,bhse->bhte', a, vc) + jnp.einsum('bhtd,bhde->bhte', qc * jnp.exp(b), S)
        b_last = b[:, :, -1]
        S = jnp.exp(b_last)[..., None] * S + jnp.einsum(
            'bhsd,bhse->bhde', kc * jnp.exp(b_last[:, :, None] - b), vc)
        return S, o

    S0 = jnp.zeros((bsz, h, dk, dv), jnp.float32)
    _, o = lax.scan(step, S0, (q, k, v, log_f))
    return o


def hgrn2_branch(q_raw, f_raw, i_raw, g_raw, lb, out_norm):
    bsz, s, _ = q_raw.shape
    f32 = jnp.float32
    q = jax.nn.silu(q_raw.astype(f32))
    lbf = lb.astype(f32)
    log_f = jnp.logaddexp(jnp.log(lbf), jnp.log1p(-lbf) + jax.nn.log_sigmoid(f_raw.astype(f32)))
    k = -jnp.expm1(log_f)
    v = i_raw.astype(f32)

    def to_chunks(t):
        return t.reshape(bsz, s // HG_CHUNK, HG_CHUNK, HG_HEADS, -1).transpose(1, 0, 3, 2, 4)

    o = _gla_chunked(to_chunks(q), to_chunks(k), to_chunks(v), to_chunks(log_f))
    o = o.transpose(1, 0, 3, 2, 4).reshape(bsz, s, HG_HEADS, HG_DV)
    o = o * lax.rsqrt(jnp.mean(o * o, axis=-1, keepdims=True) + EPS)
    o = o.reshape(bsz, s, HG_WIDTH) * out_norm.astype(f32)
    return o * jax.nn.silu(g_raw.astype(f32))


def causal_dwconv(x, w, b):
    out = lax.conv_general_dilated(
        x, w.astype(x.dtype)[:, None, :], window_strides=(1,), padding=[(w.shape[0] - 1, 0)],
        dimension_numbers=('NWC', 'WIO', 'NWC'), feature_group_count=x.shape[-1])
    return out + b.astype(x.dtype)


def _ssd_chunked(x, dt, A, Bm, Cm):
    bsz, s, g, r, p = x.shape
    n = Bm.shape[-1]
    c = SSD_CHUNK
    nc = s // c
    causal = jnp.tril(jnp.ones((c, c), bool))

    def chunks(t):
        return t.reshape(bsz, nc, c, *t.shape[2:]).swapaxes(0, 1)

    def step(state, inp):
        xc, dtc, Bc, Cc = inp
        cs = jnp.cumsum(dtc * A, axis=1)
        seg = jnp.where(causal[None, :, :, None, None], cs[:, :, None] - cs[:, None], -jnp.inf)
        cb = jnp.einsum('btgn,bsgn->btsg', Cc, Bc)
        w = cb[..., None] * jnp.exp(seg)
        dx = dtc[..., None] * xc
        y = jnp.einsum('btsgr,bsgrp->btgrp', w, dx)
        y = y + jnp.einsum('btgn,bgrpn->btgrp', Cc, state) * jnp.exp(cs)[..., None]
        cs_last = cs[:, -1]
        state = jnp.exp(cs_last)[..., None, None] * state + jnp.einsum(
            'bsgn,bsgrp->bgrpn', Bc, jnp.exp(cs_last[:, None] - cs)[..., None] * dx)
        return state, y

    state0 = jnp.zeros((bsz, g, r, p, n), jnp.float32)
    _, y = lax.scan(step, state0, (chunks(x), chunks(dt), chunks(Bm), chunks(Cm)))
    return y.swapaxes(0, 1).reshape(bsz, s, g, r, p)


def ssd_branch(z, xbc, dt_raw, conv_w, conv_b, dt_bias, a_log, d_skip, norm_w):
    bsz, s, _ = xbc.shape
    f32 = jnp.float32
    xbc = jax.nn.silu(causal_dwconv(xbc.astype(f32), conv_w, conv_b))
    xs = xbc[..., :SSD_INNER].reshape(bsz, s, SSD_GROUPS, SSD_HPG, SSD_HEADDIM)
    Bm = xbc[..., SSD_INNER:SSD_INNER + SSD_GROUPS * SSD_STATE].reshape(bsz, s, SSD_GROUPS, SSD_STATE)
    Cm = xbc[..., SSD_INNER + SSD_GROUPS * SSD_STATE:].reshape(bsz, s, SSD_GROUPS, SSD_STATE)
    dt = jax.nn.softplus(dt_raw.astype(f32) + dt_bias.astype(f32)).reshape(bsz, s, SSD_GROUPS, SSD_HPG)
    A = -jnp.exp(a_log.astype(f32)).reshape(SSD_GROUPS, SSD_HPG)
    y = _ssd_chunked(xs, dt, A, Bm, Cm)
    y = y + d_skip.astype(f32).reshape(SSD_GROUPS, SSD_HPG)[..., None] * xs
    y = y.reshape(bsz, s, SSD_INNER) * jax.nn.silu(z.astype(f32))
    yg = y.reshape(bsz, s, SSD_GROUPS, SSD_INNER // SSD_GROUPS)
    yg = yg * lax.rsqrt(jnp.mean(yg * yg, axis=-1, keepdims=True) + EPS)
    return yg.reshape(bsz, s, SSD_INNER) * norm_w.astype(f32)


def hybrid_mixer(h, w_in, lb, hg_out_norm, conv_w, conv_b, dt_bias, a_log, d_skip, ssd_norm,
                 gate_b, w_branch_a, w_branch_b, w_out):
    sizes = (HG_WIDTH, HG_WIDTH, HG_WIDTH, HG_WIDTH, SSD_INNER, SSD_CONV_CH, SSD_HEADS, D_MODEL, D_MODEL)
    points = [int(v) for v in np.cumsum(sizes)[:-1]]
    proj = h @ w_in
    q_raw, f_raw, i_raw, g_raw, z, xbc, dt_raw, ga_raw, gb_raw = jnp.split(proj, points, axis=-1)
    y_a = hgrn2_branch(q_raw, f_raw, i_raw, g_raw, lb, hg_out_norm) @ w_branch_a
    y_b = ssd_branch(z, xbc, dt_raw, conv_w, conv_b, dt_bias, a_log, d_skip, ssd_norm) @ w_branch_b
    gate_a = jax.nn.sigmoid(ga_raw + gate_b[:D_MODEL])
    gate_b_ = jax.nn.sigmoid(gb_raw + gate_b[D_MODEL:])
    return (gate_a * y_a + gate_b_ * y_b) @ w_out


def memory_cross_attention(h, mem_n, wq, wk, wv, wo):
    bsz, s, _ = h.shape
    m = mem_n.shape[1]
    q = (h @ wq).reshape(bsz, s, XA_HEADS, XA_HEAD_DIM)
    k = (mem_n @ wk).reshape(bsz, m, XA_HEADS, XA_HEAD_DIM)
    v = (mem_n @ wv).reshape(bsz, m, XA_HEADS, XA_HEAD_DIM)
    scores = jnp.einsum('bshd,bmhd->bhsm', q, k).astype(jnp.float32) * (XA_HEAD_DIM ** -0.5)
    probs = jax.nn.softmax(scores, axis=-1).astype(v.dtype)
    o = jnp.einsum('bhsm,bmhd->bshd', probs, v).reshape(bsz, s, D_MODEL)
    return o @ wo


def squared_relu_mlp(h, w1, w2):
    return jnp.square(jax.nn.relu(h @ w1)) @ w2


def setup_inputs(seed: int = 0) -> dict:
    key = jax.random.key(seed)
    ks = jax.random.split(key, 32)
    f32 = jnp.float32
    nrm = lambda k, shape, scale: jax.random.normal(k, shape, f32) * scale
    gain = lambda k, shape: 1.0 + 0.01 * jax.random.normal(k, shape, f32)
    dt0 = jnp.exp(jax.random.uniform(ks[10], (DEPTH, SSD_HEADS), f32) * (math.log(0.1) - math.log(0.001))
                  + math.log(0.001))
    return {
        "x": jax.random.normal(ks[0], (BATCH, SEQ, D_MODEL), f32),
        "mem": jax.random.normal(ks[1], (BATCH, MEM_LEN, D_MODEL), f32),
        "norm_mix": gain(ks[2], (DEPTH, D_MODEL)),
        "w_in": nrm(ks[3], (DEPTH, D_MODEL, IN_COLS), D_MODEL ** -0.5),
        "hg_lb_logits": nrm(ks[4], (DEPTH, HG_WIDTH), 1.0),
        "hg_out_norm": gain(ks[5], (DEPTH, HG_WIDTH)),
        "ssd_conv_w": nrm(ks[6], (DEPTH, SSD_CONV, SSD_CONV_CH), SSD_CONV ** -0.5),
        "ssd_conv_b": nrm(ks[7], (DEPTH, SSD_CONV_CH), 0.02),
        "ssd_dt_bias": dt0 + jnp.log(-jnp.expm1(-dt0)),
        "ssd_a_log": jnp.log(jax.random.uniform(ks[8], (DEPTH, SSD_HEADS), f32, 1.0, 16.0)),
        "ssd_d": gain(ks[9], (DEPTH, SSD_HEADS)),
        "ssd_norm": gain(ks[11], (DEPTH, SSD_INNER)),
        "gate_b": nrm(ks[12], (DEPTH, N_BRANCH * D_MODEL), 0.02),
        "w_branch_a": nrm(ks[13], (DEPTH, HG_WIDTH, D_MODEL), HG_WIDTH ** -0.5),
        "w_branch_b": nrm(ks[14], (DEPTH, SSD_INNER, D_MODEL), SSD_INNER ** -0.5),
        "w_out": nrm(ks[15], (DEPTH, D_MODEL, D_MODEL), D_MODEL ** -0.5),
        "norm_xattn": gain(ks[16], (DEPTH, D_MODEL)),
        "mem_norm": gain(ks[17], (D_MODEL,)),
        "xa_wq": nrm(ks[18], (DEPTH, D_MODEL, D_MODEL), D_MODEL ** -0.5),
        "xa_wk": nrm(ks[19], (DEPTH, D_MODEL, D_MODEL), D_MODEL ** -0.5),
        "xa_wv": nrm(ks[20], (DEPTH, D_MODEL, D_MODEL), D_MODEL ** -0.5),
        "xa_wo": nrm(ks[21], (DEPTH, D_MODEL, D_MODEL), D_MODEL ** -0.5),
        "norm_mlp": gain(ks[22], (DEPTH, D_MODEL)),
        "mlp_w1": nrm(ks[23], (DEPTH, D_MODEL, D_FF), D_MODEL ** -0.5),
        "mlp_w2": nrm(ks[24], (DEPTH, D_FF, D_MODEL), D_FF ** -0.5),
        "final_norm": gain(ks[25], (D_MODEL,)),
    }


def reference(x, mem, norm_mix, w_in, hg_lb_logits, hg_out_norm, ssd_conv_w, ssd_conv_b, ssd_dt_bias,
              ssd_a_log, ssd_d, ssd_norm, gate_b, w_branch_a, w_branch_b, w_out, norm_xattn, mem_norm,
              xa_wq, xa_wk, xa_wv, xa_wo, norm_mlp, mlp_w1, mlp_w2, final_norm):
    p = jax.nn.softmax(hg_lb_logits.astype(jnp.float32), axis=0)
    lb_all = jnp.cumsum(p, axis=0) - p[0]
    mem_n = rms_norm(mem, mem_norm)
    h_res = x
    for l in range(DEPTH):
        h = rms_norm(h_res, norm_mix[l])
        h_res = h_res + hybrid_mixer(h, w_in[l], lb_all[l], hg_out_norm[l], ssd_conv_w[l], ssd_conv_b[l],
                                     ssd_dt_bias[l], ssd_a_log[l], ssd_d[l], ssd_norm[l], gate_b[l],
                                     w_branch_a[l], w_branch_b[l], w_out[l])
        h = rms_norm(h_res, norm_xattn[l])
        h_res = h_res + memory_cross_attention(h, mem_n, xa_wq[l], xa_wk[l], xa_wv[l], xa_wo[l])
        h = rms_norm(h_res, norm_mlp[l])
        h_res = h_res + squared_relu_mlp(h, mlp_w1[l], mlp_w2[l])
    return rms_norm(h_res, final_norm).astype(x.dtype)
```

```python
import functools

import jax
import jax.numpy as jnp
from jax import lax
from jax.experimental import pallas as pl
from jax.experimental.pallas import tpu as pltpu

F32 = jnp.float32
BF16 = jnp.bfloat16

D_MODEL = 4096
DEPTH = 2
MEM_LEN = 256
EPS = 1e-5
HG_DK = 128
HG_HEADS = D_MODEL // HG_DK
HG_WIDTH = HG_HEADS * HG_DK
HG_CHUNK = 64
HG_SUB = 16
HG_NSUB = HG_CHUNK // HG_SUB
SSD_HEADDIM = 64
SSD_INNER = D_MODEL
SSD_HEADS = SSD_INNER // SSD_HEADDIM
SSD_GROUPS = 8
SSD_STATE = 128
SSD_CONV = 4
SSD_CHUNK = 128
SSD_CONV_CH = SSD_INNER + 2 * SSD_GROUPS * SSD_STATE
SSD_HPG = SSD_HEADS // SSD_GROUPS
SSD_GW = SSD_HPG * SSD_HEADDIM
XA_HEADS = 4
XA_HEAD_DIM = D_MODEL // XA_HEADS
D_FF = 4 * D_MODEL

LANES = 128
VMEM_LIMIT_BYTES = 56 * 1024 * 1024
NEG_INF = float("-inf")


def _params(semantics):
    return pltpu.CompilerParams(dimension_semantics=semantics, vmem_limit_bytes=VMEM_LIMIT_BYTES)


def _rmsnorm_kernel(x_ref, w_ref, o_ref):
    x = x_ref[...].astype(F32)
    y = x * lax.rsqrt(jnp.mean(x * x, axis=-1, keepdims=True) + EPS)
    o_ref[...] = (y * w_ref[...].astype(F32)).astype(o_ref.dtype)


def rmsnorm(x, w, out_dtype, rows=256):
    m, d = x.shape
    return pl.pallas_call(
        _rmsnorm_kernel,
        grid=(m // rows,),
        in_specs=[pl.BlockSpec((rows, d), lambda i: (i, 0)),
                  pl.BlockSpec((1, d), lambda i: (0, 0))],
        out_specs=pl.BlockSpec((rows, d), lambda i: (i, 0)),
        out_shape=jax.ShapeDtypeStruct((m, d), out_dtype),
        compiler_params=_params(("parallel",)),
        name="rmsnorm",
    )(x, w.reshape(1, d))


def _matmul_kernel(*refs, nk, relu2, has_gate, has_add):
    a_ref, w_ref = refs[0], refs[1]
    pos = 2
    if has_gate:
        g_ref, gb_ref = refs[pos], refs[pos + 1]
        pos += 2
    if has_add:
        add_ref = refs[pos]
        pos += 1
    o_ref = refs[pos]
    acc_ref = refs[pos + 1] if nk > 1 else None

    def finish(r):
        if has_gate:
            r = r * jax.nn.sigmoid(g_ref[...].astype(F32) + gb_ref[...])
        if relu2:
            r = jnp.square(jnp.maximum(r, 0.0))
        if has_add:
            r = r + add_ref[...].astype(F32)
        o_ref[...] = r.astype(o_ref.dtype)

    part = jnp.dot(a_ref[...], w_ref[...], preferred_element_type=F32)
    if nk == 1:
        finish(part)
    else:
        k = pl.program_id(2)

        @pl.when(k == 0)
        def _():
            acc_ref[...] = part

        @pl.when(jnp.logical_and(k > 0, k < nk - 1))
        def _():
            acc_ref[...] += part

        @pl.when(k == nk - 1)
        def _():
            finish(acc_ref[...] + part)


def matmul(a, w, *, n, w_col0=0, out_dtype=BF16, tm=1024, tn=1024, tk=4096,
           relu2=False, gate=None, add=None, name="matmul"):
    m, kdim = a.shape
    tk = min(tk, kdim)
    tn = min(tn, n)
    nk = kdim // tk
    assert m % tm == 0 and n % tn == 0 and kdim % tk == 0 and w_col0 % tn == 0
    wj0 = w_col0 // tn
    in_specs = [pl.BlockSpec((tm, tk), lambda i, j, k: (i, k)),
                pl.BlockSpec((tk, tn), lambda i, j, k: (k, j + wj0))]
    args = [a, w]
    if gate is not None:
        g, g_col0, gbias = gate
        assert g_col0 % tn == 0
        gj0 = g_col0 // tn
        in_specs += [pl.BlockSpec((tm, tn), lambda i, j, k: (i, j + gj0)),
                     pl.BlockSpec((1, tn), lambda i, j, k: (0, j))]
        args += [g, gbias.reshape(1, n).astype(F32)]
    if add is not None:
        in_specs.append(pl.BlockSpec((tm, tn), lambda i, j, k: (i, j)))
        args.append(add)
    kern = functools.partial(_matmul_kernel, nk=nk, relu2=relu2,
                             has_gate=gate is not None, has_add=add is not None)
    return pl.pallas_call(
        kern,
        grid=(m // tm, n // tn, nk),
        in_specs=in_specs,
        out_specs=pl.BlockSpec((tm, tn), lambda i, j, k: (i, j)),
        out_shape=jax.ShapeDtypeStruct((m, n), out_dtype),
        scratch_shapes=[pltpu.VMEM((tm, tn), F32)] if nk > 1 else [],
        compiler_params=_params(("parallel", "parallel", "arbitrary")),
        name=name,
    )(*args)


def _split3(x):
    hi = x.astype(BF16)
    r = x - hi.astype(F32)
    mid = r.astype(BF16)
    lo = (r - mid.astype(F32)).astype(BF16)
    return hi, mid, lo


def _cumsum_rows(tri, x):
    hi, mid, lo = _split3(x)
    d = lambda y: jnp.dot(tri, y, preferred_element_type=F32)
    return d(hi) + d(mid) + d(lo)


def _log1p_exp_neg_abs(x):
    return jnp.log1p(jnp.exp(-jnp.abs(x)))


def _softplus(x):
    return jnp.maximum(x, 0.0) + _log1p_exp_neg_abs(x)


def _log_sigmoid(x):
    return jnp.minimum(x, 0.0) - _log1p_exp_neg_abs(x)


def _logaddexp(a, c):
    return jnp.maximum(a, c) + _log1p_exp_neg_abs(a - c)


def _silu(x):
    return x * jax.nn.sigmoid(x)


def _dot_nt(a, b):
    return lax.dot_general(a, b, (((1,), (1,)), ((), ())), preferred_element_type=F32)


def _dot_tn(a, b):
    return lax.dot_general(a, b, (((0,), (0,)), ((), ())), preferred_element_type=F32)


def _hgrn2_kernel(q_ref, f_ref, i_ref, g_ref, lbl_ref, on_ref, o_ref, st_ref, *, layer, nchunks):
    C, L, NS, DK = HG_CHUNK, HG_SUB, HG_NSUB, HG_DK

    lg = lbl_ref[...].astype(F32)
    e = jnp.exp(lg - jnp.max(lg, axis=0, keepdims=True))
    p = e / jnp.sum(e, axis=0, keepdims=True)
    lb = jnp.sum(p[:layer + 1], axis=0, keepdims=True) - p[0:1]
    log_lb = jnp.log(lb)
    log_1m_lb = jnp.log1p(-lb)
    out_w = on_ref[...].astype(F32)

    row = lax.broadcasted_iota(jnp.int32, (C, C), 0)
    col = lax.broadcasted_iota(jnp.int32, (C, C), 1)
    tri = (row >= col).astype(BF16)
    t_in_sub = lax.broadcasted_iota(jnp.int32, (NS, L, DK), 1)
    a_lane = lax.broadcasted_iota(jnp.int32, (NS, L, C), 2)
    a_sub0 = lax.broadcasted_iota(jnp.int32, (NS, L, C), 0) * L

    st_ref[...] = jnp.zeros_like(st_ref)

    def chunk(c, carry):
        r0 = pl.multiple_of(c * C, C)
        qr = q_ref[0, pl.ds(r0, C), :].astype(F32)
        fr = f_ref[0, pl.ds(r0, C), :].astype(F32)
        v = i_ref[0, pl.ds(r0, C), :].astype(F32)
        gr = g_ref[0, pl.ds(r0, C), :].astype(F32)

        q = _silu(qr)
        log_f = _logaddexp(log_lb, log_1m_lb + _log_sigmoid(fr))
        kk = (1.0 - lb) * jax.nn.sigmoid(-fr)
        b = _cumsum_rows(tri, log_f)

        b4 = b.reshape(NS, L, DK)
        q4 = q.reshape(NS, L, DK)
        k4 = kk.reshape(NS, L, DK)
        b_ref = jnp.concatenate([jnp.zeros((1, 1, DK), F32), b4[:NS - 1, L - 1:L, :]], axis=0)
        q_off = (q4 * jnp.exp(b4 - b_ref)).astype(BF16)

        a_rows = [jnp.zeros((L, C), F32)]
        for n in range(1, NS):
            k_off = kk[:n * L] * jnp.exp(b_ref[n] - b[:n * L])
            k_off = jnp.concatenate([k_off, jnp.zeros((C - n * L, DK), F32)], axis=0).astype(BF16)
            a_rows.append(_dot_nt(q_off[n], k_off))
        a_off = jnp.concatenate(a_rows, axis=0)

        a_diag = jnp.zeros((NS, L, C), F32)
        for j in range(L):
            b_row = b4[:, j:j + 1, :]
            k_row = k4[:, j:j + 1, :]
            decay = jnp.exp(jnp.where(t_in_sub >= j, b4 - b_row, NEG_INF))
            colv = jnp.sum(q4 * (k_row * decay), axis=-1, keepdims=True)
            a_diag = jnp.where(a_lane == a_sub0 + j, colv, a_diag)
        a = (a_off + a_diag.reshape(C, C)).astype(BF16)

        st = st_ref[...]
        qe = (q * jnp.exp(b)).astype(BF16)
        o = jnp.dot(a, v.astype(BF16), preferred_element_type=F32) + _dot_nt(qe, st.astype(BF16))

        b_last = b[C - 1:C, :]
        k_dec = (kk * jnp.exp(b_last - b)).astype(BF16)
        st_ref[...] = st * jnp.exp(b_last) + _dot_tn(v.astype(BF16), k_dec)

        o = o * lax.rsqrt(jnp.mean(o * o, axis=-1, keepdims=True) + EPS)
        o = o * out_w * _silu(gr)
        o_ref[0, pl.ds(r0, C), :] = o.astype(o_ref.dtype)
        return carry

    lax.fori_loop(0, nchunks, chunk, 0)


def hgrn2_branch(q3, f3, ig3, lb_logits, out_norm, layer):
    bsz, s, _ = q3.shape
    g_blk0 = HG_WIDTH // HG_DK
    blk = lambda off: pl.BlockSpec((1, s, HG_DK), lambda b, h: (b, 0, h + off))
    kern = functools.partial(_hgrn2_kernel, layer=layer, nchunks=s // HG_CHUNK)
    return pl.pallas_call(
        kern,
        grid=(bsz, HG_HEADS),
        in_specs=[blk(0), blk(0), blk(0), blk(g_blk0),
                  pl.BlockSpec((DEPTH, HG_DK), lambda b, h: (0, h)),
                  pl.BlockSpec((1, HG_DK), lambda b, h: (0, h))],
        out_specs=blk(0),
        out_shape=jax.ShapeDtypeStruct((bsz, s, HG_WIDTH), BF16),
        scratch_shapes=[pltpu.VMEM((HG_DK, HG_DK), F32)],
        compiler_params=_params(("parallel", "parallel")),
        name="hgrn2",
    )(q3, f3, ig3, ig3, lb_logits, out_norm.reshape(1, HG_WIDTH))


def _ssd_kernel(z_ref, xbc_ref, dt_ref, cw_ref, cb_ref, dtb_ref, alog_ref, dsk_ref, nw_ref,
                o_ref, tail_ref, xc_ref, st_ref, dxs_ref):
    C, G, N, GW = SSD_CHUNK, SSD_GROUPS, SSD_STATE, SSD_GW
    HALF = LANES // 2
    assert SSD_HEADDIM == HALF and C == LANES

    @pl.when(pl.program_id(1) == 0)
    def _():
        tail_ref[...] = jnp.zeros_like(tail_ref)
        st_ref[...] = jnp.zeros_like(st_ref)

    row8 = lax.broadcasted_iota(jnp.int32, (8, 512), 0)
    for c0 in range(0, SSD_CONV_CH, 512):
        xin = xbc_ref[0, :, c0:c0 + 512].astype(F32)
        tail = tail_ref[:, c0:c0 + 512]
        w = cw_ref[:, c0:c0 + 512].astype(F32)
        acc = xin * w[SSD_CONV - 1:SSD_CONV] + cb_ref[:, c0:c0 + 512].astype(F32)
        for k in range(1, SSD_CONV):
            rolled = pltpu.roll(xin, k, axis=0)
            top = jnp.where(row8 < k, pltpu.roll(tail, k, axis=0), rolled[0:8])
            shifted = jnp.concatenate([top, rolled[8:]], axis=0)
            acc = acc + shifted * w[SSD_CONV - 1 - k:SSD_CONV - k]
        tail_ref[:, c0:c0 + 512] = xin[C - 8:C]
        xc_ref[:, c0:c0 + 512] = _silu(acc)

    row = lax.broadcasted_iota(jnp.int32, (C, C), 0)
    col = lax.broadcasted_iota(jnp.int32, (C, C), 1)
    causal = row >= col
    tri = causal.astype(BF16)
    lane_lo = col < HALF
    row_lo = row < HALF

    dt = _softplus(dt_ref[0].astype(F32) + dtb_ref[...].astype(F32))
    a_neg = -jnp.exp(alog_ref[...].astype(F32))
    cs = _cumsum_rows(tri, dt * a_neg)
    cs_t = cs.T
    cs_last = cs[C - 1:C, :]
    dsk = dsk_ref[...].astype(F32)

    def pair(col0, col1):
        return jnp.where(lane_lo[:col0.shape[0]], col0, col1)

    for g in range(G):
        bg = xc_ref[:, SSD_INNER + g * N:SSD_INNER + (g + 1) * N].astype(BF16)
        cg = xc_ref[:, SSD_INNER + (G + g) * N:SSD_INNER + (G + g + 1) * N].astype(BF16)
        cb = _dot_nt(cg, bg)
        st = st_ref[g]
        y_state = _dot_nt(cg, st.astype(BF16))
        ys = []
        for pr in range(SSD_HPG // 2):
            h0 = g * SSD_HPG + 2 * pr
            h1 = h0 + 1
            c0 = g * GW + pr * LANES
            xp = xc_ref[:, c0:c0 + LANES]
            dxp = pair(dt[:, h0:h0 + 1], dt[:, h1:h1 + 1]) * xp
            w0 = cb * jnp.exp(jnp.where(causal, cs[:, h0:h0 + 1] - cs_t[h0:h0 + 1, :], NEG_INF))
            w1 = cb * jnp.exp(jnp.where(causal, cs[:, h1:h1 + 1] - cs_t[h1:h1 + 1, :], NEG_INF))
            y = (jnp.dot(w0.astype(BF16), jnp.where(lane_lo, dxp, 0.0).astype(BF16), preferred_element_type=F32)
                 + jnp.dot(w1.astype(BF16), jnp.where(lane_lo, 0.0, dxp).astype(BF16), preferred_element_type=F32))
            csp = pair(cs[:, h0:h0 + 1], cs[:, h1:h1 + 1])
            cs_last_p = pair(cs_last[:, h0:h0 + 1], cs_last[:, h1:h1 + 1])
            y = y + y_state[:, pr * LANES:(pr + 1) * LANES] * jnp.exp(csp)
            y = y + pair(dsk[:, h0:h0 + 1], dsk[:, h1:h1 + 1]) * xp
            ys.append(y)
            dxs_ref[:, pr * LANES:(pr + 1) * LANES] = (jnp.exp(cs_last_p - csp) * dxp).astype(BF16)
            e_last = jnp.exp(jnp.where(row_lo[:, 0:1], cs_last[:, h0:h0 + 1], cs_last[:, h1:h1 + 1]))
            st_ref[g, pr * LANES:(pr + 1) * LANES, :] = (
                e_last * st[pr * LANES:(pr + 1) * LANES, :]
                + _dot_tn(dxs_ref[:, pr * LANES:(pr + 1) * LANES], bg))
        yg = jnp.concatenate(ys, axis=1)
        yg = yg * _silu(z_ref[0, :, g * GW:(g + 1) * GW].astype(F32))
        yg = yg * lax.rsqrt(jnp.mean(yg * yg, axis=-1, keepdims=True) + EPS)
        o_ref[0, :, g * GW:(g + 1) * GW] = (yg * nw_ref[:, g * GW:(g + 1) * GW].astype(F32)).astype(o_ref.dtype)


def ssd_branch(zx3, z_blk, xbc_blk, dt3, conv_w, conv_b, dt_bias, a_log, d_skip, norm_w):
    bsz, s, _ = zx3.shape
    C = SSD_CHUNK
    pad = lambda v: jnp.pad(v.astype(F32), (0, LANES - SSD_HEADS)).reshape(1, LANES)
    const = lambda shape: pl.BlockSpec(shape, lambda b, c: (0, 0))
    return pl.pallas_call(
        _ssd_kernel,
        grid=(bsz, s // C),
        in_specs=[pl.BlockSpec((1, C, SSD_INNER), lambda b, c: (b, c, z_blk)),
                  pl.BlockSpec((1, C, SSD_CONV_CH), lambda b, c: (b, c, xbc_blk)),
                  pl.BlockSpec((1, C, LANES), lambda b, c: (b, c, 0)),
                  const((SSD_CONV, SSD_CONV_CH)), const((1, SSD_CONV_CH)),
                  const((1, LANES)), const((1, LANES)), const((1, LANES)), const((1, SSD_INNER))],
        out_specs=pl.BlockSpec((1, C, SSD_INNER), lambda b, c: (b, c, 0)),
        out_shape=jax.ShapeDtypeStruct((bsz, s, SSD_INNER), BF16),
        scratch_shapes=[pltpu.VMEM((8, SSD_CONV_CH), F32),
                        pltpu.VMEM((C, SSD_CONV_CH), F32),
                        pltpu.VMEM((SSD_GROUPS, SSD_GW, SSD_STATE), F32),
                        pltpu.VMEM((C, SSD_GW), BF16)],
        compiler_params=_params(("parallel", "arbitrary")),
        name="ssd",
    )(zx3, zx3, dt3, conv_w, conv_b.reshape(1, SSD_CONV_CH), pad(dt_bias), pad(a_log), pad(d_skip),
      norm_w.reshape(1, SSD_INNER))


def _xattn_kernel(q_ref, k_ref, v_ref, o_ref):
    s = _dot_nt(q_ref[0], k_ref[0]) * (XA_HEAD_DIM ** -0.5)
    e = jnp.exp(s - jnp.max(s, axis=-1, keepdims=True))
    p = e / jnp.sum(e, axis=-1, keepdims=True)
    o_ref[0] = jnp.dot(p.astype(BF16), v_ref[0], preferred_element_type=F32).astype(o_ref.dtype)


def cross_attention(q3, k3, v3, ts=1024):
    bsz, s, _ = q3.shape
    m = k3.shape[1]
    ts = min(ts, s)
    return pl.pallas_call(
        _xattn_kernel,
        grid=(bsz, s // ts, XA_HEADS),
        in_specs=[pl.BlockSpec((1, ts, XA_HEAD_DIM), lambda b, i, h: (b, i, h)),
                  pl.BlockSpec((1, m, XA_HEAD_DIM), lambda b, i, h: (b, 0, h)),
                  pl.BlockSpec((1, m, XA_HEAD_DIM), lambda b, i, h: (b, 0, h))],
        out_specs=pl.BlockSpec((1, ts, XA_HEAD_DIM), lambda b, i, h: (b, i, h)),
        out_shape=jax.ShapeDtypeStruct((bsz, s, D_MODEL), BF16),
        compiler_params=_params(("parallel", "parallel", "parallel")),
        name="xattn",
    )(q3, k3, v3)


def _mixer(h, res, bsz, s, layer, w_in, lb_logits, hg_out_norm, conv_w, conv_b, dt_bias, a_log, d_skip,
           ssd_norm, gate_b, w_branch_a, w_branch_b, w_out, tm):
    c_zx = 2 * HG_WIDTH
    c_dt = 4 * HG_WIDTH + SSD_INNER + SSD_CONV_CH
    c_gate = c_dt + SSD_HEADS
    w_main = w_in[:, :c_dt].astype(BF16)
    w_dt = jnp.pad(w_in[:, c_dt:c_gate], ((0, 0), (0, LANES - SSD_HEADS))).astype(BF16)
    w_gate = w_in[:, c_gate:].astype(BF16)

    q = matmul(h, w_main, n=HG_WIDTH, w_col0=0, out_dtype=BF16, tm=tm, name="proj_q")
    f = matmul(h, w_main, n=HG_WIDTH, w_col0=HG_WIDTH, out_dtype=F32, tm=tm, name="proj_f")
    igzx = matmul(h, w_main, n=c_dt - c_zx, w_col0=c_zx, out_dtype=BF16, tm=tm, name="proj_igzx")
    dt = matmul(h, w_dt, n=LANES, out_dtype=F32, tm=tm, name="proj_dt")
    gates = matmul(h, w_gate, n=2 * D_MODEL, out_dtype=BF16, tm=tm, name="proj_gates")

    r3 = lambda t: t.reshape(bsz, s, t.shape[-1])
    y_a = hgrn2_branch(r3(q), r3(f), r3(igzx), lb_logits, hg_out_norm, layer)
    y_b = ssd_branch(r3(igzx), (2 * HG_WIDTH) // SSD_INNER, (2 * HG_WIDTH + SSD_INNER) // SSD_CONV_CH,
                     r3(dt), conv_w, conv_b, dt_bias, a_log, d_skip, ssd_norm)
    m = bsz * s
    t = matmul(y_a.reshape(m, HG_WIDTH), w_branch_a.astype(BF16), n=D_MODEL, out_dtype=F32, tm=tm,
               tn=512, gate=(gates, 0, gate_b[:D_MODEL]), name="branch_a")
    merged = matmul(y_b.reshape(m, SSD_INNER), w_branch_b.astype(BF16), n=D_MODEL, out_dtype=BF16, tm=tm,
                    tn=512, gate=(gates, D_MODEL, gate_b[D_MODEL:]), add=t, name="branch_b")
    return matmul(merged, w_out.astype(BF16), n=D_MODEL, out_dtype=F32, tm=tm, tn=512, add=res, name="mix_out")


def kernel(x, mem, norm_mix, w_in, hg_lb_logits, hg_out_norm, ssd_conv_w, ssd_conv_b, ssd_dt_bias, ssd_a_log,
           ssd_d, ssd_norm, gate_b, w_branch_a, w_branch_b, w_out, norm_xattn, mem_norm, xa_wq, xa_wk, xa_wv,
           xa_wo, norm_mlp, mlp_w1, mlp_w2, final_norm):
    bsz, s, d = x.shape
    m = bsz * s
    tm = min(1024, m)
    mm = bsz * mem.shape[1]
    mem_n = rmsnorm(mem.reshape(mm, d), mem_norm, BF16)
    res = x.reshape(m, d)
    for l in range(DEPTH):
        h = rmsnorm(res, norm_mix[l], BF16)
        res = _mixer(h, res, bsz, s, l, w_in[l], hg_lb_logits, hg_out_norm[l], ssd_conv_w[l], ssd_conv_b[l],
                     ssd_dt_bias[l], ssd_a_log[l], ssd_d[l], ssd_norm[l], gate_b[l], w_branch_a[l],
                     w_branch_b[l], w_out[l], tm)

        h = rmsnorm(res, norm_xattn[l], BF16)
        q = matmul(h, xa_wq[l].astype(BF16), n=d, out_dtype=BF16, tm=tm, name="xa_q")
        kv_tm = min(1024, mm)
        k = matmul(mem_n, xa_wk[l].astype(BF16), n=d, out_dtype=BF16, tm=kv_tm, name="xa_k")
        v = matmul(mem_n, xa_wv[l].astype(BF16), n=d, out_dtype=BF16, tm=kv_tm, name="xa_v")
        o = cross_attention(q.reshape(bsz, s, d), k.reshape(bsz, -1, d), v.reshape(bsz, -1, d))
        res = matmul(o.reshape(m, d), xa_wo[l].astype(BF16), n=d, out_dtype=F32, tm=tm, tn=512, add=res, name="xa_o")

        h = rmsnorm(res, norm_mlp[l], BF16)
        u = matmul(h, mlp_w1[l].astype(BF16), n=D_FF, out_dtype=BF16, tm=tm, relu2=True, name="mlp_up")
        res = matmul(u, mlp_w2[l].astype(BF16), n=d, out_dtype=F32, tm=tm, tn=512, add=res, name="mlp_down")
    return rmsnorm(res, final_norm, x.dtype).reshape(bsz, s, d)
```

```python
import functools

import numpy as np
import jax
import jax.numpy as jnp
from jax import lax
from jax.experimental import pallas as pl
from jax.experimental.pallas import tpu as pltpu

F32 = jnp.float32
BF16 = jnp.bfloat16

D_MODEL = 4096
DEPTH = 2
MEM_LEN = 256
EPS = 1e-5
HG_DK = 128
HG_HEADS = D_MODEL // HG_DK
HG_WIDTH = HG_HEADS * HG_DK
HG_CHUNK = 64
HG_NLEV = 6
HG_HEADS_PER_STEP = 4
HG_SEQ_TILE = 1024
SSD_HEADDIM = 64
SSD_INNER = D_MODEL
SSD_HEADS = SSD_INNER // SSD_HEADDIM
SSD_GROUPS = 8
SSD_STATE = 128
SSD_CONV = 4
SSD_CHUNK = 128
SSD_CONV_CH = SSD_INNER + 2 * SSD_GROUPS * SSD_STATE
SSD_HPG = SSD_HEADS // SSD_GROUPS
SSD_GW = SSD_HPG * SSD_HEADDIM
XA_HEADS = 4
XA_HEAD_DIM = D_MODEL // XA_HEADS
D_FF = 4 * D_MODEL
IN_COLS = 4 * HG_WIDTH + SSD_INNER + SSD_CONV_CH + SSD_HEADS + 2 * D_MODEL

LANES = 128
VMEM_LIMIT_BYTES = 56 * 1024 * 1024
VMEM_TILE_BUDGET_BYTES = 46 * 1024 * 1024
NEG_INF = float("-inf")


def _params(semantics):
    return pltpu.CompilerParams(dimension_semantics=semantics, vmem_limit_bytes=VMEM_LIMIT_BYTES)


def _rmsnorm_kernel(x_ref, w_ref, o_ref):
    x = x_ref[...].astype(F32)
    y = x * lax.rsqrt(jnp.mean(x * x, axis=-1, keepdims=True) + EPS)
    o_ref[...] = (y * w_ref[...].astype(F32)).astype(o_ref.dtype)


def rmsnorm(x, w, out_dtype, rows=256):
    m, d = x.shape
    return pl.pallas_call(
        _rmsnorm_kernel,
        grid=(m // rows,),
        in_specs=[pl.BlockSpec((rows, d), lambda i: (i, 0)),
                  pl.BlockSpec((1, d), lambda i: (0, 0))],
        out_specs=pl.BlockSpec((rows, d), lambda i: (i, 0)),
        out_shape=jax.ShapeDtypeStruct((m, d), out_dtype),
        compiler_params=_params(("parallel",)),
        name="rmsnorm",
    )(x, w.reshape(1, d))


def _matmul_kernel(*refs, nk, relu2, has_gate, has_add):
    a_ref, w_ref = refs[0], refs[1]
    pos = 2
    if has_gate:
        g_ref, gb_ref = refs[pos], refs[pos + 1]
        pos += 2
    if has_add:
        add_ref = refs[pos]
        pos += 1
    o_ref = refs[pos]
    acc_ref = refs[pos + 1] if nk > 1 else None

    def finish(r):
        if has_gate:
            r = r * jax.nn.sigmoid(g_ref[...].astype(F32) + gb_ref[...])
        if relu2:
            r = jnp.square(jnp.maximum(r, 0.0))
        if has_add:
            r = r + add_ref[...].astype(F32)
        o_ref[...] = r.astype(o_ref.dtype)

    part = jnp.dot(a_ref[...], w_ref[...].astype(BF16), preferred_element_type=F32)
    if nk == 1:
        finish(part)
    else:
        k = pl.program_id(2)

        @pl.when(k == 0)
        def _():
            acc_ref[...] = part

        @pl.when(jnp.logical_and(k > 0, k < nk - 1))
        def _():
            acc_ref[...] += part

        @pl.when(k == nk - 1)
        def _():
            finish(acc_ref[...] + part)


_TILE_PREFS = ((2048, 512, 1), (1024, 512, 2), (1024, 256, 2), (2048, 128, 1), (1024, 128, 2),
               (512, 256, 2), (256, 256, 2), (256, 128, 2), (128, 128, 2))


def _matmul_tiles(m, n, tk, nk, w_bytes, out_bytes, has_gate, has_add):
    for tm, tn, a_bufs in _TILE_PREFS:
        if m % tm or n % tn:
            continue
        if nk > 1:
            a_bufs = 2
        est = (tm * tk * 2 * a_bufs + tk * tn * w_bytes * 2 + (tk * tn * 2 if w_bytes != 2 else 0)
               + tm * tn * (2 * out_bytes + 4) + (tm * tn * 2 * 2 if has_gate else 0)
               + (tm * tn * 4 * 2 if has_add else 0) + (tm * tn * 4 if nk > 1 else 0))
        if est <= VMEM_TILE_BUDGET_BYTES:
            return tm, tn, a_bufs
    raise ValueError(f"no matmul tiling fits VMEM for m={m} n={n} tk={tk}")


def matmul(a, w, *, n, w_col0=0, layer=None, out_dtype=BF16, tk=4096,
           relu2=False, gate=None, add=None, name="matmul"):
    m, kdim = a.shape
    tk = min(tk, kdim)
    nk = kdim // tk
    tm, tn, a_bufs = _matmul_tiles(m, n, tk, nk, w.dtype.itemsize, jnp.dtype(out_dtype).itemsize,
                                   gate is not None, add is not None)
    assert kdim % tk == 0 and w_col0 % tn == 0
    wj0 = w_col0 // tn
    if w.ndim == 3:
        w_spec = pl.BlockSpec((None, tk, tn), lambda i, j, k: (layer, k, j + wj0))
    else:
        w_spec = pl.BlockSpec((tk, tn), lambda i, j, k: (k, j + wj0))
    a_mode = dict(pipeline_mode=pl.Buffered(1)) if a_bufs == 1 else {}
    in_specs = [pl.BlockSpec((tm, tk), lambda i, j, k: (i, k), **a_mode), w_spec]
    args = [a, w]
    if gate is not None:
        g, g_col0, gbias = gate
        assert g_col0 % tn == 0
        gj0 = g_col0 // tn
        in_specs += [pl.BlockSpec((tm, tn), lambda i, j, k: (i, j + gj0)),
                     pl.BlockSpec((1, tn), lambda i, j, k: (0, j))]
        args += [g, gbias.reshape(1, n).astype(F32)]
    if add is not None:
        in_specs.append(pl.BlockSpec((tm, tn), lambda i, j, k: (i, j)))
        args.append(add)
    kern = functools.partial(_matmul_kernel, nk=nk, relu2=relu2,
                             has_gate=gate is not None, has_add=add is not None)
    return pl.pallas_call(
        kern,
        grid=(m // tm, n // tn, nk),
        in_specs=in_specs,
        out_specs=pl.BlockSpec((tm, tn), lambda i, j, k: (i, j)),
        out_shape=jax.ShapeDtypeStruct((m, n), out_dtype),
        scratch_shapes=[pltpu.VMEM((tm, tn), F32)] if nk > 1 else [],
        compiler_params=_params(("parallel", "parallel", "arbitrary")),
        name=name,
    )(*args)


def _split3(x):
    hi = x.astype(BF16)
    r = x - hi.astype(F32)
    mid = r.astype(BF16)
    lo = (r - mid.astype(F32)).astype(BF16)
    return hi, mid, lo


def _cumsum_rows(tri, x):
    hi, mid, lo = _split3(x)
    d = lambda y: jnp.dot(tri, y, preferred_element_type=F32)
    return d(hi) + d(mid) + d(lo)


def _log1p_exp_neg_abs(x):
    return jnp.log1p(jnp.exp(-jnp.abs(x)))


def _softplus(x):
    return jnp.maximum(x, 0.0) + _log1p_exp_neg_abs(x)


def _log_sigmoid(x):
    return jnp.minimum(x, 0.0) - _log1p_exp_neg_abs(x)


def _logaddexp(a, c):
    return jnp.maximum(a, c) + _log1p_exp_neg_abs(a - c)


def _silu(x):
    return x * jax.nn.sigmoid(x)


def _dot_nt(a, b):
    return lax.dot_general(a, b, (((1,), (1,)), ((), ())), preferred_element_type=F32)


def _dot_tn(a, b):
    return lax.dot_general(a, b, (((0,), (0,)), ((), ())), preferred_element_type=F32)


def _hgrn2_kernel(q_ref, f_ref, i_ref, g_ref, lbl_ref, on_ref, sums_ref, qmask_ref, levels_ref,
                  o_ref, st_ref, *, layer, nchunks):
    C, DK = HG_CHUNK, HG_DK

    lg = lbl_ref[...].astype(F32)
    e = jnp.exp(lg - jnp.max(lg, axis=0, keepdims=True))
    p = e / jnp.sum(e, axis=0, keepdims=True)
    lb = jnp.sum(p[:layer + 1], axis=0, keepdims=True) - p[0:1]
    log_lb = jnp.log(lb)
    log_1m_lb = jnp.log1p(-lb)
    out_w = on_ref[...].astype(F32)

    levels = levels_ref[...]
    heads = [slice(hh * DK, (hh + 1) * DK) for hh in range(HG_HEADS_PER_STEP)]

    @pl.when(pl.program_id(2) == 0)
    def _():
        st_ref[...] = jnp.zeros_like(st_ref)

    def chunk(c, carry):
        r0 = pl.multiple_of(c * C, C)
        qr = q_ref[0, pl.ds(r0, C), :].astype(F32)
        fr = f_ref[0, pl.ds(r0, C), :].astype(F32)
        v = i_ref[0, pl.ds(r0, C), :].astype(BF16)
        gr = g_ref[0, pl.ds(r0, C), :].astype(F32)

        q = _silu(qr)
        log_f = _logaddexp(log_lb, log_1m_lb + _log_sigmoid(fr))
        kk = (1.0 - lb) * jax.nn.sigmoid(-fr)

        expo = jnp.dot(sums_ref[...], jnp.concatenate(_split3(log_f), axis=0), preferred_element_type=F32)
        dec = jnp.exp(expo)

        qb = q.astype(BF16)
        kb = kk.astype(BF16)
        a = [jnp.where(levels == 0, _dot_nt(qb[:, hd], kb[:, hd]), 0.0) for hd in heads]
        for lv in range(1, HG_NLEV + 1):
            isq = qmask_ref[(lv - 1) * C:lv * C, :] != 0.0
            z = (jnp.where(isq, q, kk) * dec[(lv - 1) * C:lv * C]).astype(BF16)
            a = [jnp.where(levels == lv, _dot_nt(z[:, hd], z[:, hd]), ah) for hd, ah in zip(heads, a)]

        qe = (q * dec[HG_NLEV * C:(HG_NLEV + 1) * C]).astype(BF16)
        k_dec = (kk * dec[(HG_NLEV + 1) * C:(HG_NLEV + 2) * C]).astype(BF16)
        e_last = dec[(HG_NLEV + 1) * C - 1:(HG_NLEV + 1) * C]
        outs = []
        for hh, hd in enumerate(heads):
            st = st_ref[hh]
            o = (jnp.dot(a[hh].astype(BF16), v[:, hd], preferred_element_type=F32)
                 + _dot_nt(qe[:, hd], st.astype(BF16)))
            st_ref[hh] = st * e_last[:, hd] + _dot_tn(v[:, hd], k_dec[:, hd])
            outs.append(o * lax.rsqrt(jnp.mean(o * o, axis=-1, keepdims=True) + EPS))
        o = jnp.concatenate(outs, axis=1) * out_w * _silu(gr)
        o_ref[0, pl.ds(r0, C), :] = o.astype(o_ref.dtype)
        return carry

    lax.fori_loop(0, nchunks, chunk, 0, unroll=2)


def _hgrn2_tables():
    C = HG_CHUNK
    r = np.arange(C)[:, None]
    c = np.arange(C)[None, :]
    blocks, qmasks = [], []
    levels = np.full((C, C), -1, np.int32)
    levels[r == c] = 0
    for lv in range(1, HG_NLEV + 1):
        m = C >> lv
        ref = (r // (2 * m)) * (2 * m) + m - 1
        isq = (r & m) != 0
        blocks.append(np.where(isq, (c > ref) & (c <= r), (c > r) & (c <= ref)))
        qmasks.append(np.broadcast_to(isq, (C, HG_HEADS_PER_STEP * HG_DK)))
        levels[(r > c) & (((r ^ c) >> (HG_NLEV - lv)) == 1)] = lv
    blocks.append(c <= r)
    blocks.append(c > r)
    sums = np.concatenate(blocks, axis=0).astype(np.float32)
    return (jnp.asarray(np.tile(sums, (1, 3)), BF16), jnp.asarray(np.concatenate(qmasks, 0), F32),
            jnp.asarray(levels))


def hgrn2_branch(q3, f3, ig3, lb_logits, out_norm, layer):
    bsz, s, _ = q3.shape
    w = HG_HEADS_PER_STEP * HG_DK
    ts = min(HG_SEQ_TILE, s)
    g_blk0 = HG_WIDTH // w
    blk = lambda off: pl.BlockSpec((1, ts, w), lambda b, h, t: (b, t, h + off))
    const = lambda a: pl.BlockSpec(a.shape, lambda b, h, t: (0, 0))
    sums, qmask, levels = _hgrn2_tables()
    kern = functools.partial(_hgrn2_kernel, layer=layer, nchunks=ts // HG_CHUNK)
    return pl.pallas_call(
        kern,
        grid=(bsz, HG_HEADS // HG_HEADS_PER_STEP, s // ts),
        in_specs=[blk(0), blk(0), blk(0), blk(g_blk0),
                  pl.BlockSpec((DEPTH, w), lambda b, h, t: (0, h)),
                  pl.BlockSpec((1, w), lambda b, h, t: (0, h)),
                  const(sums), const(qmask), const(levels)],
        out_specs=blk(0),
        out_shape=jax.ShapeDtypeStruct((bsz, s, HG_WIDTH), BF16),
        scratch_shapes=[pltpu.VMEM((HG_HEADS_PER_STEP, HG_DK, HG_DK), F32)],
        compiler_params=_params(("parallel", "parallel", "arbitrary")),
        name="hgrn2",
    )(q3, f3, ig3, ig3, lb_logits, out_norm.reshape(1, HG_WIDTH), sums, qmask, levels)


def _ssd_kernel(z_ref, xbc_ref, dt_ref, cw_ref, cb_ref, dtb_ref, alog_ref, dsk_ref, nw_ref,
                o_ref, tail_ref, xc_ref, st_ref, dxs_ref):
    C, G, N, GW = SSD_CHUNK, SSD_GROUPS, SSD_STATE, SSD_GW
    HALF = LANES // 2
    assert SSD_HEADDIM == HALF and C == LANES

    @pl.when(pl.program_id(1) == 0)
    def _():
        tail_ref[...] = jnp.zeros_like(tail_ref)
        st_ref[...] = jnp.zeros_like(st_ref)

    row8 = lax.broadcasted_iota(jnp.int32, (8, 512), 0)
    for c0 in range(0, SSD_CONV_CH, 512):
        xin = xbc_ref[0, :, c0:c0 + 512].astype(F32)
        tail = tail_ref[:, c0:c0 + 512]
        w = cw_ref[:, c0:c0 + 512].astype(F32)
        acc = xin * w[SSD_CONV - 1:SSD_CONV] + cb_ref[:, c0:c0 + 512].astype(F32)
        for k in range(1, SSD_CONV):
            rolled = pltpu.roll(xin, k, axis=0)
            top = jnp.where(row8 < k, pltpu.roll(tail, k, axis=0), rolled[0:8])
            shifted = jnp.concatenate([top, rolled[8:]], axis=0)
            acc = acc + shifted * w[SSD_CONV - 1 - k:SSD_CONV - k]
        tail_ref[:, c0:c0 + 512] = xin[C - 8:C]
        xc_ref[:, c0:c0 + 512] = _silu(acc)

    row = lax.broadcasted_iota(jnp.int32, (C, C), 0)
    col = lax.broadcasted_iota(jnp.int32, (C, C), 1)
    causal = row >= col
    tri = causal.astype(BF16)
    lane_lo = col < HALF
    row_lo = row < HALF

    dt = _softplus(dt_ref[0].astype(F32) + dtb_ref[...].astype(F32))
    a_neg = -jnp.exp(alog_ref[...].astype(F32))
    cs = _cumsum_rows(tri, dt * a_neg)
    cs_t = cs.T
    cs_last = cs[C - 1:C, :]
    dsk = dsk_ref[...].astype(F32)

    def pair(col0, col1):
        return jnp.where(lane_lo[:col0.shape[0]], col0, col1)

    for g in range(G):
        bg = xc_ref[:, SSD_INNER + g * N:SSD_INNER + (g + 1) * N].astype(BF16)
        cg = xc_ref[:, SSD_INNER + (G + g) * N:SSD_INNER + (G + g + 1) * N].astype(BF16)
        cb = _dot_nt(cg, bg)
        st = st_ref[g]
        y_state = _dot_nt(cg, st.astype(BF16))
        ys = []
        for pr in range(SSD_HPG // 2):
            h0 = g * SSD_HPG + 2 * pr
            h1 = h0 + 1
            c0 = g * GW + pr * LANES
            xp = xc_ref[:, c0:c0 + LANES]
            dxp = pair(dt[:, h0:h0 + 1], dt[:, h1:h1 + 1]) * xp
            w0 = cb * jnp.exp(jnp.where(causal, cs[:, h0:h0 + 1] - cs_t[h0:h0 + 1, :], NEG_INF))
            w1 = cb * jnp.exp(jnp.where(causal, cs[:, h1:h1 + 1] - cs_t[h1:h1 + 1, :], NEG_INF))
            y = (jnp.dot(w0.astype(BF16), jnp.where(lane_lo, dxp, 0.0).astype(BF16), preferred_element_type=F32)
                 + jnp.dot(w1.astype(BF16), jnp.where(lane_lo, 0.0, dxp).astype(BF16), preferred_element_type=F32))
            csp = pair(cs[:, h0:h0 + 1], cs[:, h1:h1 + 1])
            cs_last_p = pair(cs_last[:, h0:h0 + 1], cs_last[:, h1:h1 + 1])
            y = y + y_state[:, pr * LANES:(pr + 1) * LANES] * jnp.exp(csp)
            y = y + pair(dsk[:, h0:h0 + 1], dsk[:, h1:h1 + 1]) * xp
            ys.append(y)
            dxs_ref[:, pr * LANES:(pr + 1) * LANES] = (jnp.exp(cs_last_p - csp) * dxp).astype(BF16)
            e_last = jnp.exp(jnp.where(row_lo[:, 0:1], cs_last[:, h0:h0 + 1], cs_last[:, h1:h1 + 1]))
            st_ref[g, pr * LANES:(pr + 1) * LANES, :] = (
                e_last * st[pr * LANES:(pr + 1) * LANES, :]
                + _dot_tn(dxs_ref[:, pr * LANES:(pr + 1) * LANES], bg))
        yg = jnp.concatenate(ys, axis=1)
        yg = yg * _silu(z_ref[0, :, g * GW:(g + 1) * GW].astype(F32))
        yg = yg * lax.rsqrt(jnp.mean(yg * yg, axis=-1, keepdims=True) + EPS)
        o_ref[0, :, g * GW:(g + 1) * GW] = (yg * nw_ref[:, g * GW:(g + 1) * GW].astype(F32)).astype(o_ref.dtype)


def ssd_branch(zx3, z_blk, xbc_blk, dt3, conv_w, conv_b, dt_bias, a_log, d_skip, norm_w):
    bsz, s, _ = zx3.shape
    C = SSD_CHUNK
    pad = lambda v: jnp.pad(v.astype(F32), (0, LANES - SSD_HEADS)).reshape(1, LANES)
    const = lambda shape: pl.BlockSpec(shape, lambda b, c: (0, 0))
    return pl.pallas_call(
        _ssd_kernel,
        grid=(bsz, s // C),
        in_specs=[pl.BlockSpec((1, C, SSD_INNER), lambda b, c: (b, c, z_blk)),
                  pl.BlockSpec((1, C, SSD_CONV_CH), lambda b, c: (b, c, xbc_blk)),
                  pl.BlockSpec((1, C, LANES), lambda b, c: (b, c, 0)),
                  const((SSD_CONV, SSD_CONV_CH)), const((1, SSD_CONV_CH)),
                  const((1, LANES)), const((1, LANES)), const((1, LANES)), const((1, SSD_INNER))],
        out_specs=pl.BlockSpec((1, C, SSD_INNER), lambda b, c: (b, c, 0)),
        out_shape=jax.ShapeDtypeStruct((bsz, s, SSD_INNER), BF16),
        scratch_shapes=[pltpu.VMEM((8, SSD_CONV_CH), F32),
                        pltpu.VMEM((C, SSD_CONV_CH), F32),
                        pltpu.VMEM((SSD_GROUPS, SSD_GW, SSD_STATE), F32),
                        pltpu.VMEM((C, SSD_GW), BF16)],
        compiler_params=_params(("parallel", "arbitrary")),
        name="ssd",
    )(zx3, zx3, dt3, conv_w, conv_b.reshape(1, SSD_CONV_CH), pad(dt_bias), pad(a_log), pad(d_skip),
      norm_w.reshape(1, SSD_INNER))


def _xattn_kernel(q_ref, k_ref, v_ref, o_ref):
    s = _dot_nt(q_ref[0], k_ref[0]) * (XA_HEAD_DIM ** -0.5)
    e = jnp.exp(s - jnp.max(s, axis=-1, keepdims=True))
    p = e / jnp.sum(e, axis=-1, keepdims=True)
    o_ref[0] = jnp.dot(p.astype(BF16), v_ref[0], preferred_element_type=F32).astype(o_ref.dtype)


def cross_attention(q3, k3, v3, ts=1024):
    bsz, s, _ = q3.shape
    m = k3.shape[1]
    ts = min(ts, s)
    return pl.pallas_call(
        _xattn_kernel,
        grid=(bsz, s // ts, XA_HEADS),
        in_specs=[pl.BlockSpec((1, ts, XA_HEAD_DIM), lambda b, i, h: (b, i, h)),
                  pl.BlockSpec((1, m, XA_HEAD_DIM), lambda b, i, h: (b, 0, h)),
                  pl.BlockSpec((1, m, XA_HEAD_DIM), lambda b, i, h: (b, 0, h))],
        out_specs=pl.BlockSpec((1, ts, XA_HEAD_DIM), lambda b, i, h: (b, i, h)),
        out_shape=jax.ShapeDtypeStruct((bsz, s, D_MODEL), BF16),
        compiler_params=_params(("parallel", "parallel", "parallel")),
        name="xattn",
    )(q3, k3, v3)


def _mixer(h, res, bsz, s, layer, w_in, w_gate, lb_logits, hg_out_norm, conv_w, conv_b, dt_bias, a_log, d_skip,
           ssd_norm, gate_b, w_branch_a, w_branch_b, w_out):
    c_zx = 2 * HG_WIDTH
    c_dt = 4 * HG_WIDTH + SSD_INNER + SSD_CONV_CH
    q = matmul(h, w_in, layer=layer, n=HG_WIDTH, w_col0=0, out_dtype=BF16, name="proj_q")
    f = matmul(h, w_in, layer=layer, n=HG_WIDTH, w_col0=HG_WIDTH, out_dtype=F32, name="proj_f")
    igzx = matmul(h, w_in, layer=layer, n=c_dt - c_zx, w_col0=c_zx, out_dtype=BF16, name="proj_igzx")
    dt = matmul(h, w_in, layer=layer, n=LANES, w_col0=c_dt, out_dtype=F32, name="proj_dt")
    gates = matmul(h, w_gate, layer=layer, n=2 * D_MODEL, out_dtype=BF16, name="proj_gates")

    r3 = lambda t: t.reshape(bsz, s, t.shape[-1])
    y_a = hgrn2_branch(r3(q), r3(f), r3(igzx), lb_logits, hg_out_norm, layer)
    y_b = ssd_branch(r3(igzx), (2 * HG_WIDTH) // SSD_INNER, (2 * HG_WIDTH + SSD_INNER) // SSD_CONV_CH,
                     r3(dt), conv_w, conv_b, dt_bias, a_log, d_skip, ssd_norm)
    m = bsz * s
    t = matmul(y_a.reshape(m, HG_WIDTH), w_branch_a, layer=layer, n=D_MODEL, out_dtype=F32,
               gate=(gates, 0, gate_b[:D_MODEL]), name="branch_a")
    merged = matmul(y_b.reshape(m, SSD_INNER), w_branch_b, layer=layer, n=D_MODEL, out_dtype=BF16,
                    gate=(gates, D_MODEL, gate_b[D_MODEL:]), add=t, name="branch_b")
    return matmul(merged, w_out, layer=layer, n=D_MODEL, out_dtype=F32, add=res, name="mix_out")


def kernel(x, mem, norm_mix, w_in, hg_lb_logits, hg_out_norm, ssd_conv_w, ssd_conv_b, ssd_dt_bias, ssd_a_log,
           ssd_d, ssd_norm, gate_b, w_branch_a, w_branch_b, w_out, norm_xattn, mem_norm, xa_wq, xa_wk, xa_wv,
           xa_wo, norm_mlp, mlp_w1, mlp_w2, final_norm):
    bsz, s, d = x.shape
    m = bsz * s
    mm = bsz * mem.shape[1]
    mem_n = rmsnorm(mem.reshape(mm, d), mem_norm, BF16)
    w_gate = w_in[:, :, IN_COLS - 2 * D_MODEL:].astype(BF16)
    w2 = mlp_w2.astype(BF16)
    res = x.reshape(m, d)
    for l in range(DEPTH):
        h = rmsnorm(res, norm_mix[l], BF16)
        res = _mixer(h, res, bsz, s, l, w_in, w_gate, hg_lb_logits, hg_out_norm[l], ssd_conv_w[l], ssd_conv_b[l],
                     ssd_dt_bias[l], ssd_a_log[l], ssd_d[l], ssd_norm[l], gate_b[l], w_branch_a, w_branch_b, w_out)

        h = rmsnorm(res, norm_xattn[l], BF16)
        q = matmul(h, xa_wq, layer=l, n=d, out_dtype=BF16, name="xa_q")
        k = matmul(mem_n, xa_wk, layer=l, n=d, out_dtype=BF16, name="xa_k")
        v = matmul(mem_n, xa_wv, layer=l, n=d, out_dtype=BF16, name="xa_v")
        o = cross_attention(q.reshape(bsz, s, d), k.reshape(bsz, -1, d), v.reshape(bsz, -1, d))
        res = matmul(o.reshape(m, d), xa_wo, layer=l, n=d, out_dtype=F32, add=res, name="xa_o")

        h = rmsnorm(res, norm_mlp[l], BF16)
        u = matmul(h, mlp_w1, layer=l, n=D_FF, out_dtype=BF16, relu2=True, name="mlp_up")
        res = matmul(u, w2, layer=l, n=d, out_dtype=F32, add=res, name="mlp_down")
    return rmsnorm(res, final_norm, x.dtype).reshape(bsz, s, d)
```

```python
import functools

import numpy as np
import jax
import jax.numpy as jnp
from jax import lax
from jax.experimental import pallas as pl
from jax.experimental.pallas import tpu as pltpu

F32 = jnp.float32
BF16 = jnp.bfloat16

D_MODEL = 4096
DEPTH = 2
MEM_LEN = 256
EPS = 1e-5
HG_DK = 128
HG_HEADS = D_MODEL // HG_DK
HG_WIDTH = HG_HEADS * HG_DK
HG_CHUNK = 64
HG_NLEV = 6
HG_HEADS_PER_STEP = 4
HG_SEQ_TILE = 1024
SSD_HEADDIM = 64
SSD_INNER = D_MODEL
SSD_HEADS = SSD_INNER // SSD_HEADDIM
SSD_GROUPS = 8
SSD_STATE = 128
SSD_CONV = 4
SSD_CHUNK = 128
SSD_CONV_CH = SSD_INNER + 2 * SSD_GROUPS * SSD_STATE
SSD_HPG = SSD_HEADS // SSD_GROUPS
SSD_GW = SSD_HPG * SSD_HEADDIM
XA_HEADS = 4
XA_HEAD_DIM = D_MODEL // XA_HEADS
D_FF = 4 * D_MODEL
IN_COLS = 4 * HG_WIDTH + SSD_INNER + SSD_CONV_CH + SSD_HEADS + 2 * D_MODEL

LANES = 128
VMEM_LIMIT_BYTES = 56 * 1024 * 1024
VMEM_TILE_BUDGET_BYTES = 46 * 1024 * 1024
NEG_INF = float("-inf")


def _params(semantics):
    return pltpu.CompilerParams(dimension_semantics=semantics, vmem_limit_bytes=VMEM_LIMIT_BYTES)


def _rmsnorm_kernel(x_ref, w_ref, o_ref):
    x = x_ref[...].astype(F32)
    y = x * lax.rsqrt(jnp.mean(x * x, axis=-1, keepdims=True) + EPS)
    o_ref[...] = (y * w_ref[...].astype(F32)).astype(o_ref.dtype)


def rmsnorm(x, w, out_dtype, rows=256):
    m, d = x.shape
    return pl.pallas_call(
        _rmsnorm_kernel,
        grid=(m // rows,),
        in_specs=[pl.BlockSpec((rows, d), lambda i: (i, 0)),
                  pl.BlockSpec((1, d), lambda i: (0, 0))],
        out_specs=pl.BlockSpec((rows, d), lambda i: (i, 0)),
        out_shape=jax.ShapeDtypeStruct((m, d), out_dtype),
        compiler_params=_params(("parallel",)),
        name="rmsnorm",
    )(x, w.reshape(1, d))


def _dot_nt(a, b):
    return lax.dot_general(a, b, (((1,), (1,)), ((), ())), preferred_element_type=F32)


def _dot_tn(a, b):
    return lax.dot_general(a, b, (((0,), (0,)), ((), ())), preferred_element_type=F32)


def _matmul_kernel(*refs, nk, w_is_nk, relu2, has_gate, has_add):
    a_ref, w_ref = refs[0], refs[1]
    pos = 2
    if has_gate:
        g_ref, gb_ref = refs[pos], refs[pos + 1]
        pos += 2
    if has_add:
        add_ref = refs[pos]
        pos += 1
    o_ref = refs[pos]

    def finish(r):
        if has_gate:
            r = r * jax.nn.sigmoid(g_ref[...].astype(F32) + gb_ref[...])
        if relu2:
            r = jnp.square(jnp.maximum(r, 0.0))
        if has_add:
            r = r + add_ref[...].astype(F32)
        o_ref[...] = r.astype(o_ref.dtype)

    w = w_ref[...].astype(BF16)
    part = _dot_nt(a_ref[...], w) if w_is_nk else jnp.dot(a_ref[...], w, preferred_element_type=F32)
    if nk == 1:
        finish(part)
    else:
        @pl.when(pl.program_id(2) == 0)
        def _():
            o_ref[...] = add_ref[...].astype(F32) if has_add else jnp.zeros_like(o_ref)

        o_ref[...] += part


_TILE_PREFS = ((2048, 512, 1), (1024, 512, 2), (1024, 256, 2), (2048, 128, 1), (1024, 128, 2),
               (512, 256, 2), (256, 256, 2), (256, 128, 2), (128, 128, 2))
_TILE_PREFS_KSPLIT = ((1024, 1024, 2),) + _TILE_PREFS
MATMUL_TK = 4096
MATMUL_TK_SPLIT = 2048


def _matmul_tiles(m, n, tk, nk, w_bytes, out_bytes, has_gate, has_add):
    for tm, tn, a_bufs in (_TILE_PREFS if nk == 1 else _TILE_PREFS_KSPLIT):
        if m % tm or n % tn:
            continue
        if nk > 1:
            a_bufs = 2
        est = (tm * tk * 2 * a_bufs + tk * tn * w_bytes * 2 + (tk * tn * 2 if w_bytes != 2 else 0)
               + tm * tn * (2 * out_bytes + 4) + (tm * tn * 2 * 2 if has_gate else 0)
               + (tm * tn * 4 * 2 if has_add else 0))
        if est <= VMEM_TILE_BUDGET_BYTES:
            return tm, tn, a_bufs
    raise ValueError(f"no matmul tiling fits VMEM for m={m} n={n} tk={tk}")


def matmul(a, w, *, n, w_col0=0, layer=None, w_is_nk=False, out_dtype=BF16,
           relu2=False, gate=None, add=None, name="matmul"):
    m, kdim = a.shape
    tk = MATMUL_TK if kdim <= MATMUL_TK else MATMUL_TK_SPLIT
    tk = min(tk, kdim)
    nk = kdim // tk
    tm, tn, a_bufs = _matmul_tiles(m, n, tk, nk, w.dtype.itemsize, jnp.dtype(out_dtype).itemsize,
                                   gate is not None, add is not None)
    assert kdim % tk == 0 and w_col0 % tn == 0
    assert nk == 1 or (jnp.dtype(out_dtype) == F32 and gate is None and not relu2)
    wj0 = w_col0 // tn
    lead = () if w.ndim == 2 else (None,)
    if w_is_nk:
        w_idx = (lambda i, j, k: (j + wj0, k)) if w.ndim == 2 else (lambda i, j, k: (layer, j + wj0, k))
        w_spec = pl.BlockSpec(lead + (tn, tk), w_idx)
    else:
        w_idx = (lambda i, j, k: (k, j + wj0)) if w.ndim == 2 else (lambda i, j, k: (layer, k, j + wj0))
        w_spec = pl.BlockSpec(lead + (tk, tn), w_idx)
    a_mode = dict(pipeline_mode=pl.Buffered(1)) if a_bufs == 1 else {}
    in_specs = [pl.BlockSpec((tm, tk), lambda i, j, k: (i, k), **a_mode), w_spec]
    args = [a, w]
    if gate is not None:
        g, g_col0, gbias = gate
        assert g_col0 % tn == 0
        gj0 = g_col0 // tn
        in_specs += [pl.BlockSpec((tm, tn), lambda i, j, k: (i, j + gj0)),
                     pl.BlockSpec((1, tn), lambda i, j, k: (0, j))]
        args += [g, gbias.reshape(1, n).astype(F32)]
    if add is not None:
        in_specs.append(pl.BlockSpec((tm, tn), lambda i, j, k: (i, j)))
        args.append(add)
    kern = functools.partial(_matmul_kernel, nk=nk, w_is_nk=w_is_nk, relu2=relu2,
                             has_gate=gate is not None, has_add=add is not None)
    return pl.pallas_call(
        kern,
        grid=(m // tm, n // tn, nk),
        in_specs=in_specs,
        out_specs=pl.BlockSpec((tm, tn), lambda i, j, k: (i, j)),
        out_shape=jax.ShapeDtypeStruct((m, n), out_dtype),
        compiler_params=_params(("parallel", "parallel", "arbitrary")),
        name=name,
    )(*args)


def _split3(x):
    hi = x.astype(BF16)
    r = x - hi.astype(F32)
    mid = r.astype(BF16)
    lo = (r - mid.astype(F32)).astype(BF16)
    return hi, mid, lo


def _cumsum_rows(tri, x):
    hi, mid, lo = _split3(x)
    d = lambda y: jnp.dot(tri, y, preferred_element_type=F32)
    return d(hi) + d(mid) + d(lo)


def _softplus(x):
    return jnp.maximum(x, 0.0) + jnp.log1p(jnp.exp(-jnp.abs(x)))


def _log_1p_exp_neg_abs(x):
    return jnp.log(1.0 + jnp.exp(-jnp.abs(x)))


def _log_sigmoid(x):
    return jnp.minimum(x, 0.0) - _log_1p_exp_neg_abs(x)


def _logaddexp(a, c):
    return jnp.maximum(a, c) + _log_1p_exp_neg_abs(a - c)


def _sigmoid(x):
    return 1.0 / (1.0 + jnp.exp(-x))


def _silu(x):
    return x * _sigmoid(x)


def _hgrn2_kernel(q_ref, f_ref, i_ref, g_ref, lbl_ref, on_ref, sums_ref, qmask_ref, levels_ref,
                  o_ref, st_ref, *, layer, nchunks):
    C, DK = HG_CHUNK, HG_DK

    lg = lbl_ref[...].astype(F32)
    e = jnp.exp(lg - jnp.max(lg, axis=0, keepdims=True))
    p = e / jnp.sum(e, axis=0, keepdims=True)
    lb = jnp.sum(p[:layer + 1], axis=0, keepdims=True) - p[0:1]
    log_lb = jnp.log(lb)
    log_1m_lb = jnp.log1p(-lb)
    out_w = on_ref[...].astype(F32)

    levels = levels_ref[...]
    heads = [slice(hh * DK, (hh + 1) * DK) for hh in range(HG_HEADS_PER_STEP)]

    @pl.when(pl.program_id(2) == 0)
    def _():
        st_ref[...] = jnp.zeros_like(st_ref)

    def chunk(c, carry):
        r0 = pl.multiple_of(c * C, C)
        qr = q_ref[0, pl.ds(r0, C), :].astype(F32)
        fr = f_ref[0, pl.ds(r0, C), :].astype(F32)
        v = i_ref[0, pl.ds(r0, C), :].astype(BF16)
        gr = g_ref[0, pl.ds(r0, C), :].astype(F32)

        q = _silu(qr)
        log_f = _logaddexp(log_lb, log_1m_lb + _log_sigmoid(fr))
        kk = (1.0 - lb) * _sigmoid(-fr)

        expo = jnp.dot(sums_ref[...], jnp.concatenate(_split3(log_f), axis=0), preferred_element_type=F32)
        dec = jnp.exp(expo)

        qb = q.astype(BF16)
        kb = kk.astype(BF16)
        a = [jnp.where(levels == 0, _dot_nt(qb[:, hd], kb[:, hd]), 0.0) for hd in heads]
        for lv in range(1, HG_NLEV + 1):
            isq = qmask_ref[(lv - 1) * C:lv * C, :] != 0.0
            z = (jnp.where(isq, q, kk) * dec[(lv - 1) * C:lv * C]).astype(BF16)
            a = [jnp.where(levels == lv, _dot_nt(z[:, hd], z[:, hd]), ah) for hd, ah in zip(heads, a)]

        qe = (q * dec[HG_NLEV * C:(HG_NLEV + 1) * C]).astype(BF16)
        k_dec = (kk * dec[(HG_NLEV + 1) * C:(HG_NLEV + 2) * C]).astype(BF16)
        e_last = dec[(HG_NLEV + 1) * C - 1:(HG_NLEV + 1) * C]
        outs = []
        for hh, hd in enumerate(heads):
            st = st_ref[hh]
            o = (jnp.dot(a[hh].astype(BF16), v[:, hd], preferred_element_type=F32)
                 + _dot_nt(qe[:, hd], st.astype(BF16)))
            st_ref[hh] = st * e_last[:, hd] + _dot_tn(v[:, hd], k_dec[:, hd])
            outs.append(o * lax.rsqrt(jnp.mean(o * o, axis=-1, keepdims=True) + EPS))
        o = jnp.concatenate(outs, axis=1) * out_w * _silu(gr)
        o_ref[0, pl.ds(r0, C), :] = o.astype(o_ref.dtype)
        return carry

    lax.fori_loop(0, nchunks, chunk, 0, unroll=2)


def _hgrn2_tables():
    C = HG_CHUNK
    r = np.arange(C)[:, None]
    c = np.arange(C)[None, :]
    blocks, qmasks = [], []
    levels = np.full((C, C), -1, np.int32)
    levels[r == c] = 0
    for lv in range(1, HG_NLEV + 1):
        m = C >> lv
        ref = (r // (2 * m)) * (2 * m) + m - 1
        isq = (r & m) != 0
        blocks.append(np.where(isq, (c > ref) & (c <= r), (c > r) & (c <= ref)))
        qmasks.append(np.broadcast_to(isq, (C, HG_HEADS_PER_STEP * HG_DK)))
        levels[(r > c) & (((r ^ c) >> (HG_NLEV - lv)) == 1)] = lv
    blocks.append(c <= r)
    blocks.append(c > r)
    sums = np.concatenate(blocks, axis=0).astype(np.float32)
    return (jnp.asarray(np.tile(sums, (1, 3)), BF16), jnp.asarray(np.concatenate(qmasks, 0), F32),
            jnp.asarray(levels))


def hgrn2_branch(q3, f3, ig3, lb_logits, out_norm, layer):
    bsz, s, _ = q3.shape
    w = HG_HEADS_PER_STEP * HG_DK
    ts = min(HG_SEQ_TILE, s)
    g_blk0 = HG_WIDTH // w
    blk = lambda off: pl.BlockSpec((1, ts, w), lambda b, h, t: (b, t, h + off))
    const = lambda a: pl.BlockSpec(a.shape, lambda b, h, t: (0, 0))
    sums, qmask, levels = _hgrn2_tables()
    kern = functools.partial(_hgrn2_kernel, layer=layer, nchunks=ts // HG_CHUNK)
    return pl.pallas_call(
        kern,
        grid=(bsz, HG_HEADS // HG_HEADS_PER_STEP, s // ts),
        in_specs=[blk(0), blk(0), blk(0), blk(g_blk0),
                  pl.BlockSpec((DEPTH, w), lambda b, h, t: (0, h)),
                  pl.BlockSpec((1, w), lambda b, h, t: (0, h)),
                  const(sums), const(qmask), const(levels)],
        out_specs=blk(0),
        out_shape=jax.ShapeDtypeStruct((bsz, s, HG_WIDTH), BF16),
        scratch_shapes=[pltpu.VMEM((HG_HEADS_PER_STEP, HG_DK, HG_DK), F32)],
        compiler_params=_params(("parallel", "parallel", "arbitrary")),
        name="hgrn2",
    )(q3, f3, ig3, ig3, lb_logits, out_norm.reshape(1, HG_WIDTH), sums, qmask, levels)


def _ssd_kernel(z_ref, xbc_ref, dt_ref, cw_ref, cb_ref, dtb_ref, alog_ref, dsk_ref, nw_ref,
                o_ref, tail_ref, xc_ref, st_ref, dxs_ref):
    C, G, N, GW = SSD_CHUNK, SSD_GROUPS, SSD_STATE, SSD_GW
    HALF = LANES // 2
    assert SSD_HEADDIM == HALF and C == LANES

    @pl.when(pl.program_id(1) == 0)
    def _():
        tail_ref[...] = jnp.zeros_like(tail_ref)
        st_ref[...] = jnp.zeros_like(st_ref)

    row8 = lax.broadcasted_iota(jnp.int32, (8, 512), 0)
    for c0 in range(0, SSD_CONV_CH, 512):
        xin = xbc_ref[0, :, c0:c0 + 512].astype(F32)
        tail = tail_ref[:, c0:c0 + 512]
        w = cw_ref[:, c0:c0 + 512].astype(F32)
        acc = xin * w[SSD_CONV - 1:SSD_CONV] + cb_ref[:, c0:c0 + 512].astype(F32)
        for k in range(1, SSD_CONV):
            rolled = pltpu.roll(xin, k, axis=0)
            top = jnp.where(row8 < k, pltpu.roll(tail, k, axis=0), rolled[0:8])
            shifted = jnp.concatenate([top, rolled[8:]], axis=0)
            acc = acc + shifted * w[SSD_CONV - 1 - k:SSD_CONV - k]
        tail_ref[:, c0:c0 + 512] = xin[C - 8:C]
        xc_ref[:, c0:c0 + 512] = _silu(acc)

    row = lax.broadcasted_iota(jnp.int32, (C, C), 0)
    col = lax.broadcasted_iota(jnp.int32, (C, C), 1)
    causal = row >= col
    tri = causal.astype(BF16)
    lane_lo = col < HALF
    row_lo = row < HALF

    dt = _softplus(dt_ref[0].astype(F32) + dtb_ref[...].astype(F32))
    a_neg = -jnp.exp(alog_ref[...].astype(F32))
    cs = _cumsum_rows(tri, dt * a_neg)
    cs_t = cs.T
    cs_last = cs[C - 1:C, :]
    dsk = dsk_ref[...].astype(F32)

    def pair(col0, col1):
        return jnp.where(lane_lo[:col0.shape[0]], col0, col1)

    for g in range(G):
        bg = xc_ref[:, SSD_INNER + g * N:SSD_INNER + (g + 1) * N].astype(BF16)
        cg = xc_ref[:, SSD_INNER + (G + g) * N:SSD_INNER + (G + g + 1) * N].astype(BF16)
        cb = _dot_nt(cg, bg)
        st = st_ref[g]
        y_state = _dot_nt(cg, st.astype(BF16))
        ys = []
        for pr in range(SSD_HPG // 2):
            h0 = g * SSD_HPG + 2 * pr
            h1 = h0 + 1
            c0 = g * GW + pr * LANES
            xp = xc_ref[:, c0:c0 + LANES]
            dxp = pair(dt[:, h0:h0 + 1], dt[:, h1:h1 + 1]) * xp
            w0 = cb * jnp.exp(jnp.where(causal, cs[:, h0:h0 + 1] - cs_t[h0:h0 + 1, :], NEG_INF))
            w1 = cb * jnp.exp(jnp.where(causal, cs[:, h1:h1 + 1] - cs_t[h1:h1 + 1, :], NEG_INF))
            y = (jnp.dot(w0.astype(BF16), jnp.where(lane_lo, dxp, 0.0).astype(BF16), preferred_element_type=F32)
                 + jnp.dot(w1.astype(BF16), jnp.where(lane_lo, 0.0, dxp).astype(BF16), preferred_element_type=F32))
            csp = pair(cs[:, h0:h0 + 1], cs[:, h1:h1 + 1])
            cs_last_p = pair(cs_last[:, h0:h0 + 1], cs_last[:, h1:h1 + 1])
            y = y + y_state[:, pr * LANES:(pr + 1) * LANES] * jnp.exp(csp)
            y = y + pair(dsk[:, h0:h0 + 1], dsk[:, h1:h1 + 1]) * xp
            ys.append(y)
            dxs_ref[:, pr * LANES:(pr + 1) * LANES] = (jnp.exp(cs_last_p - csp) * dxp).astype(BF16)
            e_last = jnp.exp(jnp.where(row_lo[:, 0:1], cs_last[:, h0:h0 + 1], cs_last[:, h1:h1 + 1]))
            st_ref[g, pr * LANES:(pr + 1) * LANES, :] = (
                e_last * st[pr * LANES:(pr + 1) * LANES, :]
                + _dot_tn(dxs_ref[:, pr * LANES:(pr + 1) * LANES], bg))
        yg = jnp.concatenate(ys, axis=1)
        yg = yg * _silu(z_ref[0, :, g * GW:(g + 1) * GW].astype(F32))
        yg = yg * lax.rsqrt(jnp.mean(yg * yg, axis=-1, keepdims=True) + EPS)
        o_ref[0, :, g * GW:(g + 1) * GW] = (yg * nw_ref[:, g * GW:(g + 1) * GW].astype(F32)).astype(o_ref.dtype)


def ssd_branch(zx3, z_blk, xbc_blk, dt3, conv_w, conv_b, dt_bias, a_log, d_skip, norm_w):
    bsz, s, _ = zx3.shape
    C = SSD_CHUNK
    pad = lambda v: jnp.pad(v.astype(F32), (0, LANES - SSD_HEADS)).reshape(1, LANES)
    const = lambda shape: pl.BlockSpec(shape, lambda b, c: (0, 0))
    return pl.pallas_call(
        _ssd_kernel,
        grid=(bsz, s // C),
        in_specs=[pl.BlockSpec((1, C, SSD_INNER), lambda b, c: (b, c, z_blk)),
                  pl.BlockSpec((1, C, SSD_CONV_CH), lambda b, c: (b, c, xbc_blk)),
                  pl.BlockSpec((1, C, LANES), lambda b, c: (b, c, 0)),
                  const((SSD_CONV, SSD_CONV_CH)), const((1, SSD_CONV_CH)),
                  const((1, LANES)), const((1, LANES)), const((1, LANES)), const((1, SSD_INNER))],
        out_specs=pl.BlockSpec((1, C, SSD_INNER), lambda b, c: (b, c, 0)),
        out_shape=jax.ShapeDtypeStruct((bsz, s, SSD_INNER), BF16),
        scratch_shapes=[pltpu.VMEM((8, SSD_CONV_CH), F32),
                        pltpu.VMEM((C, SSD_CONV_CH), F32),
                        pltpu.VMEM((SSD_GROUPS, SSD_GW, SSD_STATE), F32),
                        pltpu.VMEM((C, SSD_GW), BF16)],
        compiler_params=_params(("parallel", "arbitrary")),
        name="ssd",
    )(zx3, zx3, dt3, conv_w, conv_b.reshape(1, SSD_CONV_CH), pad(dt_bias), pad(a_log), pad(d_skip),
      norm_w.reshape(1, SSD_INNER))


def _xattn_kernel(q_ref, k_ref, v_ref, o_ref):
    s = _dot_nt(q_ref[0], k_ref[0]) * (XA_HEAD_DIM ** -0.5)
    e = jnp.exp(s - jnp.max(s, axis=-1, keepdims=True))
    p = e / jnp.sum(e, axis=-1, keepdims=True)
    o_ref[0] = jnp.dot(p.astype(BF16), v_ref[0], preferred_element_type=F32).astype(o_ref.dtype)


def cross_attention(q3, k3, v3, ts=1024):
    bsz, s, _ = q3.shape
    m = k3.shape[1]
    ts = min(ts, s)
    return pl.pallas_call(
        _xattn_kernel,
        grid=(bsz, s // ts, XA_HEADS),
        in_specs=[pl.BlockSpec((1, ts, XA_HEAD_DIM), lambda b, i, h: (b, i, h)),
                  pl.BlockSpec((1, m, XA_HEAD_DIM), lambda b, i, h: (b, 0, h)),
                  pl.BlockSpec((1, m, XA_HEAD_DIM), lambda b, i, h: (b, 0, h))],
        out_specs=pl.BlockSpec((1, ts, XA_HEAD_DIM), lambda b, i, h: (b, i, h)),
        out_shape=jax.ShapeDtypeStruct((bsz, s, D_MODEL), BF16),
        compiler_params=_params(("parallel", "parallel", "parallel")),
        name="xattn",
    )(q3, k3, v3)


def _mixer(h, res, bsz, s, layer, w_in, w_gate, lb_logits, hg_out_norm, conv_w, conv_b, dt_bias, a_log, d_skip,
           ssd_norm, gate_b, w_branch_a, w_branch_b, w_out):
    c_zx = 2 * HG_WIDTH
    c_dt = 4 * HG_WIDTH + SSD_INNER + SSD_CONV_CH
    proj = functools.partial(matmul, h, layer=layer, w_is_nk=True)
    q = proj(w_in, n=HG_WIDTH, w_col0=0, out_dtype=BF16, name="proj_q")
    f = proj(w_in, n=HG_WIDTH, w_col0=HG_WIDTH, out_dtype=F32, name="proj_f")
    igzx = proj(w_in, n=c_dt - c_zx, w_col0=c_zx, out_dtype=BF16, name="proj_igzx")
    dt = proj(w_in, n=LANES, w_col0=c_dt, out_dtype=F32, name="proj_dt")
    gates = proj(w_gate, n=2 * D_MODEL, out_dtype=BF16, name="proj_gates")

    r3 = lambda t: t.reshape(bsz, s, t.shape[-1])
    y_a = hgrn2_branch(r3(q), r3(f), r3(igzx), lb_logits, hg_out_norm, layer)
    y_b = ssd_branch(r3(igzx), (2 * HG_WIDTH) // SSD_INNER, (2 * HG_WIDTH + SSD_INNER) // SSD_CONV_CH,
                     r3(dt), conv_w, conv_b, dt_bias, a_log, d_skip, ssd_norm)
    m = bsz * s
    t = matmul(y_a.reshape(m, HG_WIDTH), w_branch_a, layer=layer, n=D_MODEL, out_dtype=F32,
               gate=(gates, 0, gate_b[:D_MODEL]), name="branch_a")
    merged = matmul(y_b.reshape(m, SSD_INNER), w_branch_b, layer=layer, n=D_MODEL, out_dtype=BF16,
                    gate=(gates, D_MODEL, gate_b[D_MODEL:]), add=t, name="branch_b")
    return matmul(merged, w_out, layer=layer, n=D_MODEL, out_dtype=F32, add=res, name="mix_out")


def kernel(x, mem, norm_mix, w_in, hg_lb_logits, hg_out_norm, ssd_conv_w, ssd_conv_b, ssd_dt_bias, ssd_a_log,
           ssd_d, ssd_norm, gate_b, w_branch_a, w_branch_b, w_out, norm_xattn, mem_norm, xa_wq, xa_wk, xa_wv,
           xa_wo, norm_mlp, mlp_w1, mlp_w2, final_norm):
    bsz, s, d = x.shape
    m = bsz * s
    mm = bsz * mem.shape[1]
    mem_n = rmsnorm(mem.reshape(mm, d), mem_norm, BF16)
    w_in_t = jnp.swapaxes(w_in, 1, 2)
    w_gate = w_in_t[:, IN_COLS - 2 * D_MODEL:, :].astype(BF16)
    w2 = mlp_w2.astype(BF16)
    res = x.reshape(m, d)
    for l in range(DEPTH):
        h = rmsnorm(res, norm_mix[l], BF16)
        res = _mixer(h, res, bsz, s, l, w_in_t, w_gate, hg_lb_logits, hg_out_norm[l], ssd_conv_w[l], ssd_conv_b[l],
                     ssd_dt_bias[l], ssd_a_log[l], ssd_d[l], ssd_norm[l], gate_b[l], w_branch_a, w_branch_b, w_out)

        h = rmsnorm(res, norm_xattn[l], BF16)
        q = matmul(h, xa_wq, layer=l, n=d, out_dtype=BF16, name="xa_q")
        k = matmul(mem_n, xa_wk, layer=l, n=d, out_dtype=BF16, name="xa_k")
        v = matmul(mem_n, xa_wv, layer=l, n=d, out_dtype=BF16, name="xa_v")
        o = cross_attention(q.reshape(bsz, s, d), k.reshape(bsz, -1, d), v.reshape(bsz, -1, d))
        res = matmul(o.reshape(m, d), xa_wo, layer=l, n=d, out_dtype=F32, add=res, name="xa_o")

        h = rmsnorm(res, norm_mlp[l], BF16)
        u = matmul(h, mlp_w1, layer=l, n=D_FF, out_dtype=BF16, relu2=True, name="mlp_up")
        res = matmul(u, w2, layer=l, n=d, out_dtype=F32, add=res, name="mlp_down")
    return rmsnorm(res, final_norm, x.dtype).reshape(bsz, s, d)
```

```python
import functools

import numpy as np
import jax
import jax.numpy as jnp
from jax import lax
from jax.experimental import pallas as pl
from jax.experimental.pallas import tpu as pltpu

F32 = jnp.float32
BF16 = jnp.bfloat16

D_MODEL = 4096
DEPTH = 2
MEM_LEN = 256
EPS = 1e-5
HG_DK = 128
HG_HEADS = D_MODEL // HG_DK
HG_WIDTH = HG_HEADS * HG_DK
HG_CHUNK = 64
HG_NLEV = 6
HG_HEADS_PER_STEP = 4
HG_SEQ_TILE = 1024
SSD_HEADDIM = 64
SSD_INNER = D_MODEL
SSD_HEADS = SSD_INNER // SSD_HEADDIM
SSD_GROUPS = 8
SSD_STATE = 128
SSD_CONV = 4
SSD_CHUNK = 128
SSD_CONV_CH = SSD_INNER + 2 * SSD_GROUPS * SSD_STATE
SSD_HPG = SSD_HEADS // SSD_GROUPS
SSD_GW = SSD_HPG * SSD_HEADDIM
XA_HEADS = 4
XA_HEAD_DIM = D_MODEL // XA_HEADS
D_FF = 4 * D_MODEL
IN_COLS = 4 * HG_WIDTH + SSD_INNER + SSD_CONV_CH + SSD_HEADS + 2 * D_MODEL

LANES = 128
MXU_COLS = 256
VMEM_LIMIT_BYTES = 56 * 1024 * 1024
VMEM_TILE_BUDGET_BYTES = 46 * 1024 * 1024
NEG_INF = float("-inf")


def _params(semantics):
    return pltpu.CompilerParams(dimension_semantics=semantics, vmem_limit_bytes=VMEM_LIMIT_BYTES)


def _rmsnorm_kernel(x_ref, w_ref, o_ref):
    x = x_ref[...].astype(F32)
    y = x * lax.rsqrt(jnp.mean(x * x, axis=-1, keepdims=True) + EPS)
    o_ref[...] = (y * w_ref[...].astype(F32)).astype(o_ref.dtype)


def rmsnorm(x, w, out_dtype, rows=256):
    m, d = x.shape
    return pl.pallas_call(
        _rmsnorm_kernel,
        grid=(m // rows,),
        in_specs=[pl.BlockSpec((rows, d), lambda i: (i, 0)),
                  pl.BlockSpec((1, d), lambda i: (0, 0))],
        out_specs=pl.BlockSpec((rows, d), lambda i: (i, 0)),
        out_shape=jax.ShapeDtypeStruct((m, d), out_dtype),
        compiler_params=_params(("parallel",)),
        name="rmsnorm",
    )(x, w.reshape(1, d))


def _dot_nt(a, b):
    return lax.dot_general(a, b, (((1,), (1,)), ((), ())), preferred_element_type=F32)


def _dot_tn(a, b):
    return lax.dot_general(a, b, (((0,), (0,)), ((), ())), preferred_element_type=F32)


def _matmul_kernel(*refs, nk, w_is_nk, relu2, has_gate, has_add):
    a_ref, w_ref = refs[0], refs[1]
    pos = 2
    if has_gate:
        g_ref, gb_ref = refs[pos], refs[pos + 1]
        pos += 2
    if has_add:
        add_ref = refs[pos]
        pos += 1
    o_ref = refs[pos]

    def finish(r, c0, c1):
        if has_gate:
            r = r * jax.nn.sigmoid(g_ref[:, c0:c1].astype(F32) + gb_ref[:, c0:c1])
        if relu2:
            r = jnp.square(jnp.maximum(r, 0.0))
        if has_add:
            r = r + add_ref[:, c0:c1].astype(F32)
        o_ref[:, c0:c1] = r.astype(o_ref.dtype)

    def dot(c0, c1):
        if w_is_nk:
            return _dot_nt(a_ref[...], w_ref[c0:c1, :].astype(BF16))
        return jnp.dot(a_ref[...], w_ref[:, c0:c1].astype(BF16), preferred_element_type=F32)

    if nk > 1:
        @pl.when(pl.program_id(2) == 0)
        def _():
            o_ref[...] = add_ref[...].astype(F32) if has_add else jnp.zeros_like(o_ref)

    strip = min(MXU_COLS, o_ref.shape[1])
    for c0 in range(0, o_ref.shape[1], strip):
        if nk == 1:
            finish(dot(c0, c0 + strip), c0, c0 + strip)
        else:
            o_ref[:, c0:c0 + strip] += dot(c0, c0 + strip)


_TILE_PREFS = ((2048, 512, 1), (1024, 512, 2), (1024, 256, 2), (2048, 128, 1), (1024, 128, 2),
               (512, 256, 2), (256, 256, 2), (256, 128, 2), (128, 128, 2))
_TILE_PREFS_KSPLIT = ((1024, 1024, 2),) + _TILE_PREFS
MATMUL_TK = 4096
MATMUL_TK_SPLIT = 2048


def _matmul_tiles(m, n, tk, nk, w_bytes, out_bytes, has_gate, has_add):
    for tm, tn, a_bufs in (_TILE_PREFS if nk == 1 else _TILE_PREFS_KSPLIT):
        if m % tm or n % tn:
            continue
        if nk > 1:
            a_bufs = 2
        est = (tm * tk * 2 * a_bufs + tk * tn * w_bytes * 2 + (tk * tn * 2 if w_bytes != 2 else 0)
               + tm * tn * (2 * out_bytes + 4) + (tm * tn * 2 * 2 if has_gate else 0)
               + (tm * tn * 4 * 2 if has_add else 0))
        if est <= VMEM_TILE_BUDGET_BYTES:
            return tm, tn, a_bufs
    raise ValueError(f"no matmul tiling fits VMEM for m={m} n={n} tk={tk}")


def matmul(a, w, *, n, w_col0=0, layer=None, w_is_nk=False, out_dtype=BF16,
           relu2=False, gate=None, add=None, name="matmul"):
    m, kdim = a.shape
    tk = MATMUL_TK if kdim <= MATMUL_TK else MATMUL_TK_SPLIT
    tk = min(tk, kdim)
    nk = kdim // tk
    tm, tn, a_bufs = _matmul_tiles(m, n, tk, nk, w.dtype.itemsize, jnp.dtype(out_dtype).itemsize,
                                   gate is not None, add is not None)
    assert kdim % tk == 0 and w_col0 % tn == 0
    assert nk == 1 or (jnp.dtype(out_dtype) == F32 and gate is None and not relu2)
    wj0 = w_col0 // tn
    lead = () if w.ndim == 2 else (None,)
    if w_is_nk:
        w_idx = (lambda i, j, k: (j + wj0, k)) if w.ndim == 2 else (lambda i, j, k: (layer, j + wj0, k))
        w_spec = pl.BlockSpec(lead + (tn, tk), w_idx)
    else:
        w_idx = (lambda i, j, k: (k, j + wj0)) if w.ndim == 2 else (lambda i, j, k: (layer, k, j + wj0))
        w_spec = pl.BlockSpec(lead + (tk, tn), w_idx)
    a_mode = dict(pipeline_mode=pl.Buffered(1)) if a_bufs == 1 else {}
    in_specs = [pl.BlockSpec((tm, tk), lambda i, j, k: (i, k), **a_mode), w_spec]
    args = [a, w]
    if gate is not None:
        g, g_col0, gbias = gate
        assert g_col0 % tn == 0
        gj0 = g_col0 // tn
        in_specs += [pl.BlockSpec((tm, tn), lambda i, j, k: (i, j + gj0)),
                     pl.BlockSpec((1, tn), lambda i, j, k: (0, j))]
        args += [g, gbias.reshape(1, n).astype(F32)]
    if add is not None:
        in_specs.append(pl.BlockSpec((tm, tn), lambda i, j, k: (i, j)))
        args.append(add)
    kern = functools.partial(_matmul_kernel, nk=nk, w_is_nk=w_is_nk, relu2=relu2,
                             has_gate=gate is not None, has_add=add is not None)
    return pl.pallas_call(
        kern,
        grid=(m // tm, n // tn, nk),
        in_specs=in_specs,
        out_specs=pl.BlockSpec((tm, tn), lambda i, j, k: (i, j)),
        out_shape=jax.ShapeDtypeStruct((m, n), out_dtype),
        compiler_params=_params(("parallel", "parallel", "arbitrary")),
        name=name,
    )(*args)


def _split3(x):
    hi = x.astype(BF16)
    r = x - hi.astype(F32)
    mid = r.astype(BF16)
    lo = (r - mid.astype(F32)).astype(BF16)
    return hi, mid, lo


def _cumsum_rows(tri, x):
    hi, mid, lo = _split3(x)
    d = lambda y: jnp.dot(tri, y, preferred_element_type=F32)
    return d(hi) + d(mid) + d(lo)


def _softplus(x):
    return jnp.maximum(x, 0.0) + jnp.log1p(jnp.exp(-jnp.abs(x)))


def _log_1p_exp_neg_abs(x):
    return jnp.log(1.0 + jnp.exp(-jnp.abs(x)))


def _log_sigmoid(x):
    return jnp.minimum(x, 0.0) - _log_1p_exp_neg_abs(x)


def _logaddexp(a, c):
    return jnp.maximum(a, c) + _log_1p_exp_neg_abs(a - c)


def _sigmoid(x):
    return 1.0 / (1.0 + jnp.exp(-x))


def _silu(x):
    return x * _sigmoid(x)


def _hgrn2_kernel(q_ref, f_ref, i_ref, g_ref, lbl_ref, on_ref, sums_ref, qmask_ref, levels_ref,
                  o_ref, st_ref, *, layer, nchunks):
    C, DK = HG_CHUNK, HG_DK

    lg = lbl_ref[...].astype(F32)
    e = jnp.exp(lg - jnp.max(lg, axis=0, keepdims=True))
    p = e / jnp.sum(e, axis=0, keepdims=True)
    lb = jnp.sum(p[:layer + 1], axis=0, keepdims=True) - p[0:1]
    log_lb = jnp.log(lb)
    log_1m_lb = jnp.log1p(-lb)
    out_w = on_ref[...].astype(F32)

    levels = levels_ref[...]
    heads = [slice(hh * DK, (hh + 1) * DK) for hh in range(HG_HEADS_PER_STEP)]

    @pl.when(pl.program_id(2) == 0)
    def _():
        st_ref[...] = jnp.zeros_like(st_ref)

    def chunk(c, carry):
        r0 = pl.multiple_of(c * C, C)
        qr = q_ref[0, pl.ds(r0, C), :].astype(F32)
        fr = f_ref[0, pl.ds(r0, C), :].astype(F32)
        v = i_ref[0, pl.ds(r0, C), :].astype(BF16)
        gr = g_ref[0, pl.ds(r0, C), :].astype(F32)

        q = _silu(qr)
        log_f = _logaddexp(log_lb, log_1m_lb + _log_sigmoid(fr))
        kk = (1.0 - lb) * _sigmoid(-fr)

        expo = jnp.dot(sums_ref[...], jnp.concatenate(_split3(log_f), axis=0), preferred_element_type=F32)
        dec = jnp.exp(expo)

        qb = q.astype(BF16)
        kb = kk.astype(BF16)
        a = [jnp.where(levels == 0, _dot_nt(qb[:, hd], kb[:, hd]), 0.0) for hd in heads]
        for lv in range(1, HG_NLEV + 1):
            isq = qmask_ref[(lv - 1) * C:lv * C, :] != 0.0
            z = (jnp.where(isq, q, kk) * dec[(lv - 1) * C:lv * C]).astype(BF16)
            a = [jnp.where(levels == lv, _dot_nt(z[:, hd], z[:, hd]), ah) for hd, ah in zip(heads, a)]

        qe = (q * dec[HG_NLEV * C:(HG_NLEV + 1) * C]).astype(BF16)
        k_dec = (kk * dec[(HG_NLEV + 1) * C:(HG_NLEV + 2) * C]).astype(BF16)
        e_last = dec[(HG_NLEV + 1) * C - 1:(HG_NLEV + 1) * C]
        outs = []
        for hh, hd in enumerate(heads):
            st = st_ref[hh]
            o = (jnp.dot(a[hh].astype(BF16), v[:, hd], preferred_element_type=F32)
                 + _dot_nt(qe[:, hd], st.astype(BF16)))
            st_ref[hh] = st * e_last[:, hd] + _dot_tn(v[:, hd], k_dec[:, hd])
            outs.append(o * lax.rsqrt(jnp.mean(o * o, axis=-1, keepdims=True) + EPS))
        o = jnp.concatenate(outs, axis=1) * out_w * _silu(gr)
        o_ref[0, pl.ds(r0, C), :] = o.astype(o_ref.dtype)
        return carry

    lax.fori_loop(0, nchunks, chunk, 0, unroll=4)


def _hgrn2_tables():
    C = HG_CHUNK
    r = np.arange(C)[:, None]
    c = np.arange(C)[None, :]
    blocks, qmasks = [], []
    levels = np.full((C, C), -1, np.int32)
    levels[r == c] = 0
    for lv in range(1, HG_NLEV + 1):
        m = C >> lv
        ref = (r // (2 * m)) * (2 * m) + m - 1
        isq = (r & m) != 0
        blocks.append(np.where(isq, (c > ref) & (c <= r), (c > r) & (c <= ref)))
        qmasks.append(np.broadcast_to(isq, (C, HG_HEADS_PER_STEP * HG_DK)))
        levels[(r > c) & (((r ^ c) >> (HG_NLEV - lv)) == 1)] = lv
    blocks.append(c <= r)
    blocks.append(c > r)
    sums = np.concatenate(blocks, axis=0).astype(np.float32)
    return (jnp.asarray(np.tile(sums, (1, 3)), BF16), jnp.asarray(np.concatenate(qmasks, 0), F32),
            jnp.asarray(levels))


def hgrn2_branch(q3, f3, ig3, lb_logits, out_norm, layer):
    bsz, s, _ = q3.shape
    w = HG_HEADS_PER_STEP * HG_DK
    ts = min(HG_SEQ_TILE, s)
    g_blk0 = HG_WIDTH // w
    blk = lambda off: pl.BlockSpec((1, ts, w), lambda b, h, t: (b, t, h + off))
    const = lambda a: pl.BlockSpec(a.shape, lambda b, h, t: (0, 0))
    sums, qmask, levels = _hgrn2_tables()
    kern = functools.partial(_hgrn2_kernel, layer=layer, nchunks=ts // HG_CHUNK)
    return pl.pallas_call(
        kern,
        grid=(bsz, HG_HEADS // HG_HEADS_PER_STEP, s // ts),
        in_specs=[blk(0), blk(0), blk(0), blk(g_blk0),
                  pl.BlockSpec((DEPTH, w), lambda b, h, t: (0, h)),
                  pl.BlockSpec((1, w), lambda b, h, t: (0, h)),
                  const(sums), const(qmask), const(levels)],
        out_specs=blk(0),
        out_shape=jax.ShapeDtypeStruct((bsz, s, HG_WIDTH), BF16),
        scratch_shapes=[pltpu.VMEM((HG_HEADS_PER_STEP, HG_DK, HG_DK), F32)],
        compiler_params=_params(("parallel", "parallel", "arbitrary")),
        name="hgrn2",
    )(q3, f3, ig3, ig3, lb_logits, out_norm.reshape(1, HG_WIDTH), sums, qmask, levels)


def _ssd_kernel(z_ref, xbc_ref, dt_ref, cw_ref, cb_ref, dtb_ref, alog_ref, dsk_ref, nw_ref,
                o_ref, tail_ref, xc_ref, st_ref, dxs_ref):
    C, G, N, GW = SSD_CHUNK, SSD_GROUPS, SSD_STATE, SSD_GW
    HALF = LANES // 2
    assert SSD_HEADDIM == HALF and C == LANES

    @pl.when(pl.program_id(1) == 0)
    def _():
        tail_ref[...] = jnp.zeros_like(tail_ref)
        st_ref[...] = jnp.zeros_like(st_ref)

    row = lax.broadcasted_iota(jnp.int32, (C, C), 0)
    col = lax.broadcasted_iota(jnp.int32, (C, C), 1)
    causal = row >= col
    tri = causal.astype(BF16)
    lane_lo = col < HALF
    row_lo = row < HALF

    dt = _softplus(dt_ref[0].astype(F32) + dtb_ref[...].astype(F32))
    a_neg = -jnp.exp(alog_ref[...].astype(F32))
    cs = _cumsum_rows(tri, dt * a_neg)
    cs_t = cs.T
    cs_last = cs[C - 1:C, :]
    dsk = dsk_ref[...].astype(F32)

    sr = lax.broadcasted_iota(jnp.int32, (SSD_CONV * C, C), 0)
    sc = lax.broadcasted_iota(jnp.int32, (SSD_CONV * C, C), 1)
    shifts = (sc + sr // C == sr % C).astype(BF16)
    row8 = lax.broadcasted_iota(jnp.int32, (8, 512), 0)
    for c0 in range(0, SSD_CONV_CH, 512):
        taps = jnp.dot(shifts, xbc_ref[0, :, c0:c0 + 512], preferred_element_type=F32)
        tail = tail_ref[:, c0:c0 + 512]
        w = cw_ref[:, c0:c0 + 512].astype(F32)
        acc = taps[0:C] * w[SSD_CONV - 1:SSD_CONV] + cb_ref[:, c0:c0 + 512].astype(F32)
        for k in range(1, SSD_CONV):
            sk = taps[k * C:(k + 1) * C]
            top = jnp.where(row8 < k, pltpu.roll(tail, k, axis=0), sk[0:8])
            acc = acc + jnp.concatenate([top, sk[8:]], axis=0) * w[SSD_CONV - 1 - k:SSD_CONV - k]
        tail_ref[:, c0:c0 + 512] = taps[C - 8:C]
        xc_ref[:, c0:c0 + 512] = _silu(acc)

    def pair(col0, col1):
        return jnp.where(lane_lo[:col0.shape[0]], col0, col1)

    for g in range(G):
        bg = xc_ref[:, SSD_INNER + g * N:SSD_INNER + (g + 1) * N].astype(BF16)
        cg = xc_ref[:, SSD_INNER + (G + g) * N:SSD_INNER + (G + g + 1) * N].astype(BF16)
        cb = _dot_nt(cg, bg)
        st = st_ref[g]
        y_state = _dot_nt(cg, st.astype(BF16))
        ys = []
        for pr in range(SSD_HPG // 2):
            h0 = g * SSD_HPG + 2 * pr
            h1 = h0 + 1
            c0 = g * GW + pr * LANES
            xp = xc_ref[:, c0:c0 + LANES]
            dxp = pair(dt[:, h0:h0 + 1], dt[:, h1:h1 + 1]) * xp
            w0 = cb * jnp.exp(jnp.where(causal, cs[:, h0:h0 + 1] - cs_t[h0:h0 + 1, :], NEG_INF))
            w1 = cb * jnp.exp(jnp.where(causal, cs[:, h1:h1 + 1] - cs_t[h1:h1 + 1, :], NEG_INF))
            y = (jnp.dot(w0.astype(BF16), jnp.where(lane_lo, dxp, 0.0).astype(BF16), preferred_element_type=F32)
                 + jnp.dot(w1.astype(BF16), jnp.where(lane_lo, 0.0, dxp).astype(BF16), preferred_element_type=F32))
            csp = pair(cs[:, h0:h0 + 1], cs[:, h1:h1 + 1])
            cs_last_p = pair(cs_last[:, h0:h0 + 1], cs_last[:, h1:h1 + 1])
            y = y + y_state[:, pr * LANES:(pr + 1) * LANES] * jnp.exp(csp)
            y = y + pair(dsk[:, h0:h0 + 1], dsk[:, h1:h1 + 1]) * xp
            ys.append(y)
            dxs_ref[:, pr * LANES:(pr + 1) * LANES] = (jnp.exp(cs_last_p - csp) * dxp).astype(BF16)
            e_last = jnp.exp(jnp.where(row_lo[:, 0:1], cs_last[:, h0:h0 + 1], cs_last[:, h1:h1 + 1]))
            st_ref[g, pr * LANES:(pr + 1) * LANES, :] = (
                e_last * st[pr * LANES:(pr + 1) * LANES, :]
                + _dot_tn(dxs_ref[:, pr * LANES:(pr + 1) * LANES], bg))
        yg = jnp.concatenate(ys, axis=1)
        yg = yg * _silu(z_ref[0, :, g * GW:(g + 1) * GW].astype(F32))
        yg = yg * lax.rsqrt(jnp.mean(yg * yg, axis=-1, keepdims=True) + EPS)
        o_ref[0, :, g * GW:(g + 1) * GW] = (yg * nw_ref[:, g * GW:(g + 1) * GW].astype(F32)).astype(o_ref.dtype)


def ssd_branch(zx3, z_blk, xbc_blk, dt3, conv_w, conv_b, dt_bias, a_log, d_skip, norm_w):
    bsz, s, _ = zx3.shape
    C = SSD_CHUNK
    assert zx3.dtype == BF16, "the conv row shifts are exact only for bf16 input"
    pad = lambda v: jnp.pad(v.astype(F32), (0, LANES - SSD_HEADS)).reshape(1, LANES)
    const = lambda shape: pl.BlockSpec(shape, lambda b, c: (0, 0))
    return pl.pallas_call(
        _ssd_kernel,
        grid=(bsz, s // C),
        in_specs=[pl.BlockSpec((1, C, SSD_INNER), lambda b, c: (b, c, z_blk)),
                  pl.BlockSpec((1, C, SSD_CONV_CH), lambda b, c: (b, c, xbc_blk)),
                  pl.BlockSpec((1, C, LANES), lambda b, c: (b, c, 0)),
                  const((SSD_CONV, SSD_CONV_CH)), const((1, SSD_CONV_CH)),
                  const((1, LANES)), const((1, LANES)), const((1, LANES)), const((1, SSD_INNER))],
        out_specs=pl.BlockSpec((1, C, SSD_INNER), lambda b, c: (b, c, 0)),
        out_shape=jax.ShapeDtypeStruct((bsz, s, SSD_INNER), BF16),
        scratch_shapes=[pltpu.VMEM((8, SSD_CONV_CH), F32),
                        pltpu.VMEM((C, SSD_CONV_CH), F32),
                        pltpu.VMEM((SSD_GROUPS, SSD_GW, SSD_STATE), F32),
                        pltpu.VMEM((C, SSD_GW), BF16)],
        compiler_params=_params(("parallel", "arbitrary")),
        name="ssd",
    )(zx3, zx3, dt3, conv_w, conv_b.reshape(1, SSD_CONV_CH), pad(dt_bias), pad(a_log), pad(d_skip),
      norm_w.reshape(1, SSD_INNER))


def _xattn_kernel(q_ref, k_ref, v_ref, o_ref):
    s = _dot_nt(q_ref[0], k_ref[0]) * (XA_HEAD_DIM ** -0.5)
    e = jnp.exp(s - jnp.max(s, axis=-1, keepdims=True))
    p = e / jnp.sum(e, axis=-1, keepdims=True)
    o_ref[0] = jnp.dot(p.astype(BF16), v_ref[0], preferred_element_type=F32).astype(o_ref.dtype)


def cross_attention(q3, k3, v3, ts=1024):
    bsz, s, _ = q3.shape
    m = k3.shape[1]
    ts = min(ts, s)
    return pl.pallas_call(
        _xattn_kernel,
        grid=(bsz, s // ts, XA_HEADS),
        in_specs=[pl.BlockSpec((1, ts, XA_HEAD_DIM), lambda b, i, h: (b, i, h)),
                  pl.BlockSpec((1, m, XA_HEAD_DIM), lambda b, i, h: (b, 0, h)),
                  pl.BlockSpec((1, m, XA_HEAD_DIM), lambda b, i, h: (b, 0, h))],
        out_specs=pl.BlockSpec((1, ts, XA_HEAD_DIM), lambda b, i, h: (b, i, h)),
        out_shape=jax.ShapeDtypeStruct((bsz, s, D_MODEL), BF16),
        compiler_params=_params(("parallel", "parallel", "parallel")),
        name="xattn",
    )(q3, k3, v3)


def _mixer(h, res, bsz, s, layer, w_in, w_gate, lb_logits, hg_out_norm, conv_w, conv_b, dt_bias, a_log, d_skip,
           ssd_norm, gate_b, w_branch_a, w_branch_b, w_out):
    c_zx = 2 * HG_WIDTH
    c_dt = 4 * HG_WIDTH + SSD_INNER + SSD_CONV_CH
    proj = functools.partial(matmul, h, layer=layer, w_is_nk=True)
    q = proj(w_in, n=HG_WIDTH, w_col0=0, out_dtype=BF16, name="proj_q")
    f = proj(w_in, n=HG_WIDTH, w_col0=HG_WIDTH, out_dtype=F32, name="proj_f")
    igzx = proj(w_in, n=c_dt - c_zx, w_col0=c_zx, out_dtype=BF16, name="proj_igzx")
    dt = proj(w_in, n=LANES, w_col0=c_dt, out_dtype=F32, name="proj_dt")
    gates = proj(w_gate, n=2 * D_MODEL, out_dtype=BF16, name="proj_gates")

    r3 = lambda t: t.reshape(bsz, s, t.shape[-1])
    y_a = hgrn2_branch(r3(q), r3(f), r3(igzx), lb_logits, hg_out_norm, layer)
    y_b = ssd_branch(r3(igzx), (2 * HG_WIDTH) // SSD_INNER, (2 * HG_WIDTH + SSD_INNER) // SSD_CONV_CH,
                     r3(dt), conv_w, conv_b, dt_bias, a_log, d_skip, ssd_norm)
    m = bsz * s
    t = matmul(y_a.reshape(m, HG_WIDTH), w_branch_a, layer=layer, n=D_MODEL, out_dtype=F32,
               gate=(gates, 0, gate_b[:D_MODEL]), name="branch_a")
    merged = matmul(y_b.reshape(m, SSD_INNER), w_branch_b, layer=layer, n=D_MODEL, out_dtype=BF16,
                    gate=(gates, D_MODEL, gate_b[D_MODEL:]), add=t, name="branch_b")
    return matmul(merged, w_out, layer=layer, n=D_MODEL, out_dtype=F32, add=res, name="mix_out")


def kernel(x, mem, norm_mix, w_in, hg_lb_logits, hg_out_norm, ssd_conv_w, ssd_conv_b, ssd_dt_bias, ssd_a_log,
           ssd_d, ssd_norm, gate_b, w_branch_a, w_branch_b, w_out, norm_xattn, mem_norm, xa_wq, xa_wk, xa_wv,
           xa_wo, norm_mlp, mlp_w1, mlp_w2, final_norm):
    bsz, s, d = x.shape
    m = bsz * s
    mm = bsz * mem.shape[1]
    mem_n = rmsnorm(mem.reshape(mm, d), mem_norm, BF16)
    w_in_t = jnp.swapaxes(w_in, 1, 2)
    w_gate = w_in_t[:, IN_COLS - 2 * D_MODEL:, :].astype(BF16)
    w2 = mlp_w2.astype(BF16)
    res = x.reshape(m, d)
    for l in range(DEPTH):
        h = rmsnorm(res, norm_mix[l], BF16)
        res = _mixer(h, res, bsz, s, l, w_in_t, w_gate, hg_lb_logits, hg_out_norm[l], ssd_conv_w[l], ssd_conv_b[l],
                     ssd_dt_bias[l], ssd_a_log[l], ssd_d[l], ssd_norm[l], gate_b[l], w_branch_a, w_branch_b, w_out)

        h = rmsnorm(res, norm_xattn[l], BF16)
        q = matmul(h, xa_wq, layer=l, n=d, out_dtype=BF16, name="xa_q")
        k = matmul(mem_n, xa_wk, layer=l, n=d, out_dtype=BF16, name="xa_k")
        v = matmul(mem_n, xa_wv, layer=l, n=d, out_dtype=BF16, name="xa_v")
        o = cross_attention(q.reshape(bsz, s, d), k.reshape(bsz, -1, d), v.reshape(bsz, -1, d))
        res = matmul(o.reshape(m, d), xa_wo, layer=l, n=d, out_dtype=F32, add=res, name="xa_o")

        h = rmsnorm(res, norm_mlp[l], BF16)
        u = matmul(h, mlp_w1, layer=l, n=D_FF, out_dtype=BF16, relu2=True, name="mlp_up")
        res = matmul(u, w2, layer=l, n=d, out_dtype=F32, add=res, name="mlp_down")
    return rmsnorm(res, final_norm, x.dtype).reshape(bsz, s, d)
```

```python
import functools

import numpy as np
import jax
import jax.numpy as jnp
from jax import lax
from jax.experimental import pallas as pl
from jax.experimental.pallas import tpu as pltpu

F32 = jnp.float32
BF16 = jnp.bfloat16

D_MODEL = 4096
DEPTH = 2
MEM_LEN = 256
EPS = 1e-5
HG_DK = 128
HG_HEADS = D_MODEL // HG_DK
HG_WIDTH = HG_HEADS * HG_DK
HG_CHUNK = 64
HG_NLEV = 6
HG_HEADS_PER_STEP = 4
HG_SEQ_TILE = 1024
SSD_HEADDIM = 64
SSD_INNER = D_MODEL
SSD_HEADS = SSD_INNER // SSD_HEADDIM
SSD_GROUPS = 8
SSD_STATE = 128
SSD_CONV = 4
SSD_CHUNK = 128
SSD_CONV_CH = SSD_INNER + 2 * SSD_GROUPS * SSD_STATE
SSD_HPG = SSD_HEADS // SSD_GROUPS
SSD_GW = SSD_HPG * SSD_HEADDIM
XA_HEADS = 4
XA_HEAD_DIM = D_MODEL // XA_HEADS
XA_HEADS_PER_STEP = 2
D_FF = 4 * D_MODEL
IN_COLS = 4 * HG_WIDTH + SSD_INNER + SSD_CONV_CH + SSD_HEADS + 2 * D_MODEL

LANES = 128
MXU_COLS = 256
VMEM_LIMIT_BYTES = 56 * 1024 * 1024
VMEM_TILE_BUDGET_BYTES = 46 * 1024 * 1024
NEG_INF = float("-inf")


def _params(semantics):
    return pltpu.CompilerParams(dimension_semantics=semantics, vmem_limit_bytes=VMEM_LIMIT_BYTES)


def _rmsnorm_kernel(x_ref, w_ref, o_ref):
    x = x_ref[...].astype(F32)
    y = x * lax.rsqrt(jnp.mean(x * x, axis=-1, keepdims=True) + EPS)
    o_ref[...] = (y * w_ref[...].astype(F32)).astype(o_ref.dtype)


def rmsnorm(x, w, out_dtype, rows=256):
    m, d = x.shape
    return pl.pallas_call(
        _rmsnorm_kernel,
        grid=(m // rows,),
        in_specs=[pl.BlockSpec((rows, d), lambda i: (i, 0)),
                  pl.BlockSpec((1, d), lambda i: (0, 0))],
        out_specs=pl.BlockSpec((rows, d), lambda i: (i, 0)),
        out_shape=jax.ShapeDtypeStruct((m, d), out_dtype),
        compiler_params=_params(("parallel",)),
        name="rmsnorm",
    )(x, w.reshape(1, d))


def _dot_nt(a, b):
    return lax.dot_general(a, b, (((1,), (1,)), ((), ())), preferred_element_type=F32)


def _dot_tn(a, b):
    return lax.dot_general(a, b, (((0,), (0,)), ((), ())), preferred_element_type=F32)


def _matmul_kernel(*refs, nk, w_is_nk, relu2, has_gate, has_add):
    a_ref, w_ref = refs[0], refs[1]
    pos = 2
    if has_gate:
        g_ref, gb_ref = refs[pos], refs[pos + 1]
        pos += 2
    if has_add:
        add_ref = refs[pos]
        pos += 1
    o_ref = refs[pos]

    def finish(r, c0, c1):
        if has_gate:
            r = r * jax.nn.sigmoid(g_ref[:, c0:c1].astype(F32) + gb_ref[:, c0:c1])
        if relu2:
            r = jnp.square(jnp.maximum(r, 0.0))
        if has_add:
            r = r + add_ref[:, c0:c1].astype(F32)
        o_ref[:, c0:c1] = r.astype(o_ref.dtype)

    def dot(c0, c1):
        if w_is_nk:
            return _dot_nt(a_ref[...], w_ref[c0:c1, :].astype(BF16))
        return jnp.dot(a_ref[...], w_ref[:, c0:c1].astype(BF16), preferred_element_type=F32)

    if nk > 1:
        @pl.when(pl.program_id(2) == 0)
        def _():
            o_ref[...] = add_ref[...].astype(F32) if has_add else jnp.zeros_like(o_ref)

    strip = min(MXU_COLS, o_ref.shape[1])
    for c0 in range(0, o_ref.shape[1], strip):
        if nk == 1:
            finish(dot(c0, c0 + strip), c0, c0 + strip)
        else:
            o_ref[:, c0:c0 + strip] += dot(c0, c0 + strip)


_TILE_PREFS = ((2048, 512, 1), (1024, 512, 2), (1024, 256, 2), (1024, 128, 2),
               (512, 256, 2), (256, 256, 2), (256, 128, 2), (128, 128, 2))
_TILE_PREFS_KSPLIT = ((1024, 1024, 2),) + _TILE_PREFS
MATMUL_TK = 4096
MATMUL_TK_SPLIT = 2048


def _matmul_tiles(m, n, tk, nk, w_bytes, out_bytes, has_gate, has_add):
    for tm, tn, a_bufs in (_TILE_PREFS if nk == 1 else _TILE_PREFS_KSPLIT):
        if m % tm or n % tn:
            continue
        if nk > 1:
            a_bufs = 2
        est = (tm * tk * 2 * a_bufs + tk * tn * w_bytes * 2 + (tk * tn * 2 if w_bytes != 2 else 0)
               + tm * tn * (2 * out_bytes + 4) + (tm * tn * 2 * 2 if has_gate else 0)
               + (tm * tn * 4 * 2 if has_add else 0))
        if est <= VMEM_TILE_BUDGET_BYTES:
            return tm, tn, a_bufs
    raise ValueError(f"no matmul tiling fits VMEM for m={m} n={n} tk={tk}")


def matmul(a, w, *, n, w_col0=0, layer=None, w_is_nk=False, out_dtype=BF16,
           relu2=False, gate=None, add=None, name="matmul"):
    m, kdim = a.shape
    tk = MATMUL_TK if kdim <= MATMUL_TK else MATMUL_TK_SPLIT
    tk = min(tk, kdim)
    nk = kdim // tk
    tm, tn, a_bufs = _matmul_tiles(m, n, tk, nk, w.dtype.itemsize, jnp.dtype(out_dtype).itemsize,
                                   gate is not None, add is not None)
    assert kdim % tk == 0 and w_col0 % tn == 0
    assert nk == 1 or (jnp.dtype(out_dtype) == F32 and gate is None and not relu2)
    wj0 = w_col0 // tn
    lead = () if w.ndim == 2 else (None,)
    if w_is_nk:
        w_idx = (lambda i, j, k: (j + wj0, k)) if w.ndim == 2 else (lambda i, j, k: (layer, j + wj0, k))
        w_spec = pl.BlockSpec(lead + (tn, tk), w_idx)
    else:
        w_idx = (lambda i, j, k: (k, j + wj0)) if w.ndim == 2 else (lambda i, j, k: (layer, k, j + wj0))
        w_spec = pl.BlockSpec(lead + (tk, tn), w_idx)
    a_mode = dict(pipeline_mode=pl.Buffered(1)) if a_bufs == 1 else {}
    in_specs = [pl.BlockSpec((tm, tk), lambda i, j, k: (i, k), **a_mode), w_spec]
    args = [a, w]
    if gate is not None:
        g, g_col0, gbias = gate
        assert g_col0 % tn == 0
        gj0 = g_col0 // tn
        in_specs += [pl.BlockSpec((tm, tn), lambda i, j, k: (i, j + gj0)),
                     pl.BlockSpec((1, tn), lambda i, j, k: (0, j))]
        args += [g, gbias.reshape(1, n).astype(F32)]
    if add is not None:
        in_specs.append(pl.BlockSpec((tm, tn), lambda i, j, k: (i, j)))
        args.append(add)
    kern = functools.partial(_matmul_kernel, nk=nk, w_is_nk=w_is_nk, relu2=relu2,
                             has_gate=gate is not None, has_add=add is not None)
    return pl.pallas_call(
        kern,
        grid=(m // tm, n // tn, nk),
        in_specs=in_specs,
        out_specs=pl.BlockSpec((tm, tn), lambda i, j, k: (i, j)),
        out_shape=jax.ShapeDtypeStruct((m, n), out_dtype),
        compiler_params=_params(("parallel", "parallel", "arbitrary")),
        name=name,
    )(*args)


def _split3(x):
    hi = x.astype(BF16)
    r = x - hi.astype(F32)
    mid = r.astype(BF16)
    lo = (r - mid.astype(F32)).astype(BF16)
    return hi, mid, lo


def _cumsum_rows(tri, x):
    hi, mid, lo = _split3(x)
    d = lambda y: jnp.dot(tri, y, preferred_element_type=F32)
    return d(hi) + d(mid) + d(lo)


def _softplus(x):
    return jnp.maximum(x, 0.0) + jnp.log1p(jnp.exp(-jnp.abs(x)))


def _log_1p_exp_neg_abs(x):
    return jnp.log(1.0 + jnp.exp(-jnp.abs(x)))


def _log_sigmoid(x):
    return jnp.minimum(x, 0.0) - _log_1p_exp_neg_abs(x)


def _logaddexp(a, c):
    return jnp.maximum(a, c) + _log_1p_exp_neg_abs(a - c)


def _sigmoid(x):
    return 1.0 / (1.0 + jnp.exp(-x))


def _silu(x):
    return x * _sigmoid(x)


def _hgrn2_kernel(q_ref, f_ref, i_ref, g_ref, lbl_ref, on_ref, sums_ref, qmask_ref, levels_ref,
                  o_ref, st_ref, *, layer, nchunks):
    C, DK = HG_CHUNK, HG_DK

    lg = lbl_ref[...].astype(F32)
    e = jnp.exp(lg - jnp.max(lg, axis=0, keepdims=True))
    p = e / jnp.sum(e, axis=0, keepdims=True)
    lb = jnp.sum(p[:layer + 1], axis=0, keepdims=True) - p[0:1]
    log_lb = jnp.log(lb)
    log_1m_lb = jnp.log1p(-lb)
    out_w = on_ref[...].astype(F32)

    levels = levels_ref[...]
    heads = [slice(hh * DK, (hh + 1) * DK) for hh in range(HG_HEADS_PER_STEP)]

    @pl.when(pl.program_id(2) == 0)
    def _():
        st_ref[...] = jnp.zeros_like(st_ref)

    def chunk(c, carry):
        r0 = pl.multiple_of(c * C, C)
        qr = q_ref[0, pl.ds(r0, C), :].astype(F32)
        fr = f_ref[0, pl.ds(r0, C), :].astype(F32)
        v = i_ref[0, pl.ds(r0, C), :].astype(BF16)
        gr = g_ref[0, pl.ds(r0, C), :].astype(F32)

        q = _silu(qr)
        log_f = _logaddexp(log_lb, log_1m_lb + _log_sigmoid(fr))
        kk = (1.0 - lb) * _sigmoid(-fr)

        expo = jnp.dot(sums_ref[...], jnp.concatenate(_split3(log_f), axis=0), preferred_element_type=F32)
        dec = jnp.exp(expo)

        qb = q.astype(BF16)
        kb = kk.astype(BF16)
        a = [jnp.where(levels == 0, _dot_nt(qb[:, hd], kb[:, hd]), 0.0) for hd in heads]
        for lv in range(1, HG_NLEV + 1):
            isq = qmask_ref[(lv - 1) * C:lv * C, :] != 0.0
            z = (jnp.where(isq, q, kk) * dec[(lv - 1) * C:lv * C]).astype(BF16)
            a = [jnp.where(levels == lv, _dot_nt(z[:, hd], z[:, hd]), ah) for hd, ah in zip(heads, a)]

        qe = (q * dec[HG_NLEV * C:(HG_NLEV + 1) * C]).astype(BF16)
        k_dec = (kk * dec[(HG_NLEV + 1) * C:(HG_NLEV + 2) * C]).astype(BF16)
        e_last = dec[(HG_NLEV + 1) * C - 1:(HG_NLEV + 1) * C]
        outs = []
        for hh, hd in enumerate(heads):
            st = st_ref[hh]
            o = (jnp.dot(a[hh].astype(BF16), v[:, hd], preferred_element_type=F32)
                 + _dot_nt(qe[:, hd], st.astype(BF16)))
            st_ref[hh] = st * e_last[:, hd] + _dot_tn(v[:, hd], k_dec[:, hd])
            outs.append(o * lax.rsqrt(jnp.mean(o * o, axis=-1, keepdims=True) + EPS))
        o = jnp.concatenate(outs, axis=1) * out_w * _silu(gr)
        o_ref[0, pl.ds(r0, C), :] = o.astype(o_ref.dtype)
        return carry

    lax.fori_loop(0, nchunks, chunk, 0, unroll=8)


def _hgrn2_tables():
    C = HG_CHUNK
    r = np.arange(C)[:, None]
    c = np.arange(C)[None, :]
    blocks, qmasks = [], []
    levels = np.full((C, C), -1, np.int32)
    levels[r == c] = 0
    for lv in range(1, HG_NLEV + 1):
        m = C >> lv
        ref = (r // (2 * m)) * (2 * m) + m - 1
        isq = (r & m) != 0
        blocks.append(np.where(isq, (c > ref) & (c <= r), (c > r) & (c <= ref)))
        qmasks.append(np.broadcast_to(isq, (C, HG_HEADS_PER_STEP * HG_DK)))
        levels[(r > c) & (((r ^ c) >> (HG_NLEV - lv)) == 1)] = lv
    blocks.append(c <= r)
    blocks.append(c > r)
    sums = np.concatenate(blocks, axis=0).astype(np.float32)
    return (jnp.asarray(np.tile(sums, (1, 3)), BF16), jnp.asarray(np.concatenate(qmasks, 0), F32),
            jnp.asarray(levels))


def hgrn2_branch(q3, f3, ig3, lb_logits, out_norm, layer):
    bsz, s, _ = q3.shape
    w = HG_HEADS_PER_STEP * HG_DK
    ts = min(HG_SEQ_TILE, s)
    g_blk0 = HG_WIDTH // w
    blk = lambda off: pl.BlockSpec((1, ts, w), lambda b, h, t: (b, t, h + off))
    const = lambda a: pl.BlockSpec(a.shape, lambda b, h, t: (0, 0))
    sums, qmask, levels = _hgrn2_tables()
    kern = functools.partial(_hgrn2_kernel, layer=layer, nchunks=ts // HG_CHUNK)
    return pl.pallas_call(
        kern,
        grid=(bsz, HG_HEADS // HG_HEADS_PER_STEP, s // ts),
        in_specs=[blk(0), blk(0), blk(0), blk(g_blk0),
                  pl.BlockSpec((DEPTH, w), lambda b, h, t: (0, h)),
                  pl.BlockSpec((1, w), lambda b, h, t: (0, h)),
                  const(sums), const(qmask), const(levels)],
        out_specs=blk(0),
        out_shape=jax.ShapeDtypeStruct((bsz, s, HG_WIDTH), BF16),
        scratch_shapes=[pltpu.VMEM((HG_HEADS_PER_STEP, HG_DK, HG_DK), F32)],
        compiler_params=_params(("parallel", "parallel", "arbitrary")),
        name="hgrn2",
    )(q3, f3, ig3, ig3, lb_logits, out_norm.reshape(1, HG_WIDTH), sums, qmask, levels)


def _ssd_kernel(z_ref, xbc_ref, dt_ref, cw_ref, cb_ref, dtb_ref, alog_ref, dsk_ref, nw_ref,
                o_ref, tail_ref, xc_ref, st_ref, dxs_ref):
    C, G, N, GW = SSD_CHUNK, SSD_GROUPS, SSD_STATE, SSD_GW
    HALF = LANES // 2
    assert SSD_HEADDIM == HALF and C == LANES

    @pl.when(pl.program_id(1) == 0)
    def _():
        tail_ref[...] = jnp.zeros_like(tail_ref)
        st_ref[...] = jnp.zeros_like(st_ref)

    row = lax.broadcasted_iota(jnp.int32, (C, C), 0)
    col = lax.broadcasted_iota(jnp.int32, (C, C), 1)
    causal = row >= col
    tri = causal.astype(BF16)
    lane_lo = col < HALF
    row_lo = row < HALF

    dt = _softplus(dt_ref[0].astype(F32) + dtb_ref[...].astype(F32))
    a_neg = -jnp.exp(alog_ref[...].astype(F32))
    cs = _cumsum_rows(tri, dt * a_neg)
    cs_t = cs.T
    cs_last = cs[C - 1:C, :]
    dsk = dsk_ref[...].astype(F32)

    sr = lax.broadcasted_iota(jnp.int32, (SSD_CONV * C, C), 0)
    sc = lax.broadcasted_iota(jnp.int32, (SSD_CONV * C, C), 1)
    shifts = (sc + sr // C == sr % C).astype(BF16)
    row8 = lax.broadcasted_iota(jnp.int32, (8, 512), 0)
    for c0 in range(0, SSD_CONV_CH, 512):
        taps = jnp.dot(shifts, xbc_ref[0, :, c0:c0 + 512], preferred_element_type=F32)
        tail = tail_ref[:, c0:c0 + 512]
        w = cw_ref[:, c0:c0 + 512].astype(F32)
        acc = taps[0:C] * w[SSD_CONV - 1:SSD_CONV] + cb_ref[:, c0:c0 + 512].astype(F32)
        for k in range(1, SSD_CONV):
            sk = taps[k * C:(k + 1) * C]
            top = jnp.where(row8 < k, pltpu.roll(tail, k, axis=0), sk[0:8])
            acc = acc + jnp.concatenate([top, sk[8:]], axis=0) * w[SSD_CONV - 1 - k:SSD_CONV - k]
        tail_ref[:, c0:c0 + 512] = taps[C - 8:C]
        xc_ref[:, c0:c0 + 512] = _silu(acc)

    def pair(col0, col1):
        return jnp.where(lane_lo[:col0.shape[0]], col0, col1)

    for g in range(G):
        bg = xc_ref[:, SSD_INNER + g * N:SSD_INNER + (g + 1) * N].astype(BF16)
        cg = xc_ref[:, SSD_INNER + (G + g) * N:SSD_INNER + (G + g + 1) * N].astype(BF16)
        cb = _dot_nt(cg, bg)
        st = st_ref[g]
        y_state = _dot_nt(cg, st.astype(BF16))
        ys = []
        for pr in range(SSD_HPG // 2):
            h0 = g * SSD_HPG + 2 * pr
            h1 = h0 + 1
            c0 = g * GW + pr * LANES
            xp = xc_ref[:, c0:c0 + LANES]
            dxp = pair(dt[:, h0:h0 + 1], dt[:, h1:h1 + 1]) * xp
            w0 = cb * jnp.exp(jnp.where(causal, cs[:, h0:h0 + 1] - cs_t[h0:h0 + 1, :], NEG_INF))
            w1 = cb * jnp.exp(jnp.where(causal, cs[:, h1:h1 + 1] - cs_t[h1:h1 + 1, :], NEG_INF))
            y = (jnp.dot(w0.astype(BF16), jnp.where(lane_lo, dxp, 0.0).astype(BF16), preferred_element_type=F32)
                 + jnp.dot(w1.astype(BF16), jnp.where(lane_lo, 0.0, dxp).astype(BF16), preferred_element_type=F32))
            csp = pair(cs[:, h0:h0 + 1], cs[:, h1:h1 + 1])
            cs_last_p = pair(cs_last[:, h0:h0 + 1], cs_last[:, h1:h1 + 1])
            y = y + y_state[:, pr * LANES:(pr + 1) * LANES] * jnp.exp(csp)
            y = y + pair(dsk[:, h0:h0 + 1], dsk[:, h1:h1 + 1]) * xp
            ys.append(y)
            dxs_ref[:, pr * LANES:(pr + 1) * LANES] = (jnp.exp(cs_last_p - csp) * dxp).astype(BF16)
            e_last = jnp.exp(jnp.where(row_lo[:, 0:1], cs_last[:, h0:h0 + 1], cs_last[:, h1:h1 + 1]))
            st_ref[g, pr * LANES:(pr + 1) * LANES, :] = (
                e_last * st[pr * LANES:(pr + 1) * LANES, :]
                + _dot_tn(dxs_ref[:, pr * LANES:(pr + 1) * LANES], bg))
        yg = jnp.concatenate(ys, axis=1)
        yg = yg * _silu(z_ref[0, :, g * GW:(g + 1) * GW].astype(F32))
        yg = yg * lax.rsqrt(jnp.mean(yg * yg, axis=-1, keepdims=True) + EPS)
        o_ref[0, :, g * GW:(g + 1) * GW] = (yg * nw_ref[:, g * GW:(g + 1) * GW].astype(F32)).astype(o_ref.dtype)


def ssd_branch(zx3, z_blk, xbc_blk, dt3, conv_w, conv_b, dt_bias, a_log, d_skip, norm_w):
    bsz, s, _ = zx3.shape
    C = SSD_CHUNK
    assert zx3.dtype == BF16, "the conv row shifts are exact only for bf16 input"
    pad = lambda v: jnp.pad(v.astype(F32), (0, LANES - SSD_HEADS)).reshape(1, LANES)
    const = lambda shape: pl.BlockSpec(shape, lambda b, c: (0, 0))
    return pl.pallas_call(
        _ssd_kernel,
        grid=(bsz, s // C),
        in_specs=[pl.BlockSpec((1, C, SSD_INNER), lambda b, c: (b, c, z_blk)),
                  pl.BlockSpec((1, C, SSD_CONV_CH), lambda b, c: (b, c, xbc_blk)),
                  pl.BlockSpec((1, C, LANES), lambda b, c: (b, c, 0)),
                  const((SSD_CONV, SSD_CONV_CH)), const((1, SSD_CONV_CH)),
                  const((1, LANES)), const((1, LANES)), const((1, LANES)), const((1, SSD_INNER))],
        out_specs=pl.BlockSpec((1, C, SSD_INNER), lambda b, c: (b, c, 0)),
        out_shape=jax.ShapeDtypeStruct((bsz, s, SSD_INNER), BF16),
        scratch_shapes=[pltpu.VMEM((8, SSD_CONV_CH), F32),
                        pltpu.VMEM((C, SSD_CONV_CH), F32),
                        pltpu.VMEM((SSD_GROUPS, SSD_GW, SSD_STATE), F32),
                        pltpu.VMEM((C, SSD_GW), BF16)],
        compiler_params=_params(("parallel", "arbitrary")),
        name="ssd",
    )(zx3, zx3, dt3, conv_w, conv_b.reshape(1, SSD_CONV_CH), pad(dt_bias), pad(a_log), pad(d_skip),
      norm_w.reshape(1, SSD_INNER))


def _xattn_kernel(q_ref, k_ref, v_ref, o_ref):
    for hh in range(XA_HEADS_PER_STEP):
        hd = slice(hh * XA_HEAD_DIM, (hh + 1) * XA_HEAD_DIM)
        s = _dot_nt(q_ref[0, :, hd], k_ref[0, :, hd]) * (XA_HEAD_DIM ** -0.5)
        e = jnp.exp(s - jnp.max(s, axis=-1, keepdims=True))
        p = e / jnp.sum(e, axis=-1, keepdims=True)
        o_ref[0, :, hd] = jnp.dot(p.astype(BF16), v_ref[0, :, hd], preferred_element_type=F32).astype(o_ref.dtype)


def cross_attention(q3, k3, v3, ts=1024):
    bsz, s, _ = q3.shape
    m = k3.shape[1]
    ts = min(ts, s)
    w = XA_HEADS_PER_STEP * XA_HEAD_DIM
    return pl.pallas_call(
        _xattn_kernel,
        grid=(bsz, s // ts, XA_HEADS // XA_HEADS_PER_STEP),
        in_specs=[pl.BlockSpec((1, ts, w), lambda b, i, h: (b, i, h)),
                  pl.BlockSpec((1, m, w), lambda b, i, h: (b, 0, h)),
                  pl.BlockSpec((1, m, w), lambda b, i, h: (b, 0, h))],
        out_specs=pl.BlockSpec((1, ts, w), lambda b, i, h: (b, i, h)),
        out_shape=jax.ShapeDtypeStruct((bsz, s, D_MODEL), BF16),
        compiler_params=_params(("parallel", "parallel", "parallel")),
        name="xattn",
    )(q3, k3, v3)


def _mixer(h, res, bsz, s, layer, w_in, w_gate, lb_logits, hg_out_norm, conv_w, conv_b, dt_bias, a_log, d_skip,
           ssd_norm, gate_b, w_branch_a, w_branch_b, w_out):
    c_zx = 2 * HG_WIDTH
    c_dt = 4 * HG_WIDTH + SSD_INNER + SSD_CONV_CH
    proj = functools.partial(matmul, h, layer=layer, w_is_nk=True)
    q = proj(w_in, n=HG_WIDTH, w_col0=0, out_dtype=BF16, name="proj_q")
    f = proj(w_in, n=HG_WIDTH, w_col0=HG_WIDTH, out_dtype=F32, name="proj_f")
    igzx = proj(w_in, n=c_dt - c_zx, w_col0=c_zx, out_dtype=BF16, name="proj_igzx")
    dt = proj(w_in, n=LANES, w_col0=c_dt, out_dtype=F32, name="proj_dt")
    gates = proj(w_gate, n=2 * D_MODEL, out_dtype=BF16, name="proj_gates")

    r3 = lambda t: t.reshape(bsz, s, t.shape[-1])
    y_a = hgrn2_branch(r3(q), r3(f), r3(igzx), lb_logits, hg_out_norm, layer)
    y_b = ssd_branch(r3(igzx), (2 * HG_WIDTH) // SSD_INNER, (2 * HG_WIDTH + SSD_INNER) // SSD_CONV_CH,
                     r3(dt), conv_w, conv_b, dt_bias, a_log, d_skip, ssd_norm)
    m = bsz * s
    t = matmul(y_a.reshape(m, HG_WIDTH), w_branch_a, layer=layer, n=D_MODEL, out_dtype=F32,
               gate=(gates, 0, gate_b[:D_MODEL]), name="branch_a")
    merged = matmul(y_b.reshape(m, SSD_INNER), w_branch_b, layer=layer, n=D_MODEL, out_dtype=BF16,
                    gate=(gates, D_MODEL, gate_b[D_MODEL:]), add=t, name="branch_b")
    return matmul(merged, w_out, layer=layer, n=D_MODEL, out_dtype=F32, add=res, name="mix_out")


def kernel(x, mem, norm_mix, w_in, hg_lb_logits, hg_out_norm, ssd_conv_w, ssd_conv_b, ssd_dt_bias, ssd_a_log,
           ssd_d, ssd_norm, gate_b, w_branch_a, w_branch_b, w_out, norm_xattn, mem_norm, xa_wq, xa_wk, xa_wv,
           xa_wo, norm_mlp, mlp_w1, mlp_w2, final_norm):
    bsz, s, d = x.shape
    m = bsz * s
    mm = bsz * mem.shape[1]
    mem_n = rmsnorm(mem.reshape(mm, d), mem_norm, BF16)
    w_in_t = jnp.swapaxes(w_in, 1, 2)
    w_gate = w_in_t[:, IN_COLS - 2 * D_MODEL:, :].astype(BF16)
    w2 = mlp_w2.astype(BF16)
    res = x.reshape(m, d)
    for l in range(DEPTH):
        h = rmsnorm(res, norm_mix[l], BF16)
        res = _mixer(h, res, bsz, s, l, w_in_t, w_gate, hg_lb_logits, hg_out_norm[l], ssd_conv_w[l], ssd_conv_b[l],
                     ssd_dt_bias[l], ssd_a_log[l], ssd_d[l], ssd_norm[l], gate_b[l], w_branch_a, w_branch_b, w_out)

        h = rmsnorm(res, norm_xattn[l], BF16)
        q = matmul(h, xa_wq, layer=l, n=d, out_dtype=BF16, name="xa_q")
        k = matmul(mem_n, xa_wk, layer=l, n=d, out_dtype=BF16, name="xa_k")
        v = matmul(mem_n, xa_wv, layer=l, n=d, out_dtype=BF16, name="xa_v")
        o = cross_attention(q.reshape(bsz, s, d), k.reshape(bsz, -1, d), v.reshape(bsz, -1, d))
        res = matmul(o.reshape(m, d), xa_wo, layer=l, n=d, out_dtype=F32, add=res, name="xa_o")

        h = rmsnorm(res, norm_mlp[l], BF16)
        u = matmul(h, mlp_w1, layer=l, n=D_FF, out_dtype=BF16, relu2=True, name="mlp_up")
        res = matmul(u, w2, layer=l, n=d, out_dtype=F32, add=res, name="mlp_down")
    return rmsnorm(res, final_norm, x.dtype).reshape(bsz, s, d)
```

```python
import functools

import numpy as np
import jax
import jax.numpy as jnp
from jax import lax
from jax.experimental import pallas as pl
from jax.experimental.pallas import tpu as pltpu

F32 = jnp.float32
BF16 = jnp.bfloat16

D_MODEL = 4096
DEPTH = 2
MEM_LEN = 256
EPS = 1e-5
HG_DK = 128
HG_HEADS = D_MODEL // HG_DK
HG_WIDTH = HG_HEADS * HG_DK
HG_CHUNK = 64
HG_NLEV = 6
HG_HEADS_PER_STEP = 4
HG_SEQ_TILE = 1024
SSD_HEADDIM = 64
SSD_INNER = D_MODEL
SSD_HEADS = SSD_INNER // SSD_HEADDIM
SSD_GROUPS = 8
SSD_STATE = 128
SSD_CONV = 4
SSD_CHUNK = 128
SSD_CONV_CH = SSD_INNER + 2 * SSD_GROUPS * SSD_STATE
SSD_HPG = SSD_HEADS // SSD_GROUPS
SSD_GW = SSD_HPG * SSD_HEADDIM
XA_HEADS = 4
XA_HEAD_DIM = D_MODEL // XA_HEADS
XA_HEADS_PER_STEP = 2
D_FF = 4 * D_MODEL
IN_COLS = 4 * HG_WIDTH + SSD_INNER + SSD_CONV_CH + SSD_HEADS + 2 * D_MODEL

LANES = 128
MXU_COLS = 256
VMEM_LIMIT_BYTES = 56 * 1024 * 1024
VMEM_TILE_BUDGET_BYTES = 46 * 1024 * 1024
NEG_INF = float("-inf")


def _params(semantics):
    return pltpu.CompilerParams(dimension_semantics=semantics, vmem_limit_bytes=VMEM_LIMIT_BYTES)


def _rmsnorm_kernel(x_ref, w_ref, o_ref):
    x = x_ref[...].astype(F32)
    y = x * lax.rsqrt(jnp.mean(x * x, axis=-1, keepdims=True) + EPS)
    o_ref[...] = (y * w_ref[...].astype(F32)).astype(o_ref.dtype)


def rmsnorm(x, w, out_dtype, rows=256):
    m, d = x.shape
    return pl.pallas_call(
        _rmsnorm_kernel,
        grid=(m // rows,),
        in_specs=[pl.BlockSpec((rows, d), lambda i: (i, 0)),
                  pl.BlockSpec((1, d), lambda i: (0, 0))],
        out_specs=pl.BlockSpec((rows, d), lambda i: (i, 0)),
        out_shape=jax.ShapeDtypeStruct((m, d), out_dtype),
        compiler_params=_params(("parallel",)),
        name="rmsnorm",
    )(x, w.reshape(1, d))


def _dot_nt(a, b):
    return lax.dot_general(a, b, (((1,), (1,)), ((), ())), preferred_element_type=F32)


def _dot_tn(a, b):
    return lax.dot_general(a, b, (((0,), (0,)), ((), ())), preferred_element_type=F32)


def _matmul_kernel(*refs, nk, w_is_nk, relu2, has_gate, has_add):
    a_ref, w_ref = refs[0], refs[1]
    pos = 2
    if has_gate:
        g_ref, gb_ref = refs[pos], refs[pos + 1]
        pos += 2
    if has_add:
        add_ref = refs[pos]
        pos += 1
    o_ref = refs[pos]

    def finish(r, c0, c1):
        if has_gate:
            r = r * jax.nn.sigmoid(g_ref[:, c0:c1].astype(F32) + gb_ref[:, c0:c1])
        if relu2:
            r = jnp.square(jnp.maximum(r, 0.0))
        if has_add:
            r = r + add_ref[:, c0:c1].astype(F32)
        o_ref[:, c0:c1] = r.astype(o_ref.dtype)

    def dot(c0, c1):
        if w_is_nk:
            return _dot_nt(a_ref[...], w_ref[c0:c1, :].astype(BF16))
        return jnp.dot(a_ref[...], w_ref[:, c0:c1].astype(BF16), preferred_element_type=F32)

    if nk > 1:
        @pl.when(pl.program_id(2) == 0)
        def _():
            o_ref[...] = add_ref[...].astype(F32) if has_add else jnp.zeros_like(o_ref)

    strip = min(MXU_COLS, o_ref.shape[1])
    for c0 in range(0, o_ref.shape[1], strip):
        if nk == 1:
            finish(dot(c0, c0 + strip), c0, c0 + strip)
        else:
            o_ref[:, c0:c0 + strip] += dot(c0, c0 + strip)


_TILE_PREFS = ((2048, 512, 1), (1024, 512, 2), (1024, 256, 2), (1024, 128, 2),
               (512, 256, 2), (256, 256, 2), (256, 128, 2), (128, 128, 2))
_TILE_PREFS_KSPLIT = ((1024, 1024, 2),) + _TILE_PREFS
MATMUL_TK = 4096
MATMUL_TK_SPLIT = 2048


def _matmul_tiles(m, n, tk, nk, w_bytes, out_bytes, has_gate, has_add):
    for tm, tn, a_bufs in (_TILE_PREFS if nk == 1 else _TILE_PREFS_KSPLIT):
        if m % tm or n % tn:
            continue
        if nk > 1:
            a_bufs = 2
        est = (tm * tk * 2 * a_bufs + tk * tn * w_bytes * 2 + (tk * tn * 2 if w_bytes != 2 else 0)
               + tm * tn * (2 * out_bytes + 4) + (tm * tn * 2 * 2 if has_gate else 0)
               + (tm * tn * 4 * 2 if has_add else 0))
        if est <= VMEM_TILE_BUDGET_BYTES:
            return tm, tn, a_bufs
    raise ValueError(f"no matmul tiling fits VMEM for m={m} n={n} tk={tk}")


def matmul(a, w, *, n, w_col0=0, layer=None, w_is_nk=False, out_dtype=BF16,
           relu2=False, gate=None, add=None, name="matmul"):
    m, kdim = a.shape
    tk = MATMUL_TK if kdim <= MATMUL_TK else MATMUL_TK_SPLIT
    tk = min(tk, kdim)
    nk = kdim // tk
    tm, tn, a_bufs = _matmul_tiles(m, n, tk, nk, w.dtype.itemsize, jnp.dtype(out_dtype).itemsize,
                                   gate is not None, add is not None)
    assert kdim % tk == 0 and w_col0 % tn == 0
    assert nk == 1 or (jnp.dtype(out_dtype) == F32 and gate is None and not relu2)
    wj0 = w_col0 // tn
    lead = () if w.ndim == 2 else (None,)
    if w_is_nk:
        w_idx = (lambda i, j, k: (j + wj0, k)) if w.ndim == 2 else (lambda i, j, k: (layer, j + wj0, k))
        w_spec = pl.BlockSpec(lead + (tn, tk), w_idx)
    else:
        w_idx = (lambda i, j, k: (k, j + wj0)) if w.ndim == 2 else (lambda i, j, k: (layer, k, j + wj0))
        w_spec = pl.BlockSpec(lead + (tk, tn), w_idx)
    a_mode = dict(pipeline_mode=pl.Buffered(1)) if a_bufs == 1 else {}
    in_specs = [pl.BlockSpec((tm, tk), lambda i, j, k: (i, k), **a_mode), w_spec]
    args = [a, w]
    if gate is not None:
        g, g_col0, gbias = gate
        assert g_col0 % tn == 0
        gj0 = g_col0 // tn
        in_specs += [pl.BlockSpec((tm, tn), lambda i, j, k: (i, j + gj0)),
                     pl.BlockSpec((1, tn), lambda i, j, k: (0, j))]
        args += [g, gbias.reshape(1, n).astype(F32)]
    if add is not None:
        in_specs.append(pl.BlockSpec((tm, tn), lambda i, j, k: (i, j)))
        args.append(add)
    kern = functools.partial(_matmul_kernel, nk=nk, w_is_nk=w_is_nk, relu2=relu2,
                             has_gate=gate is not None, has_add=add is not None)
    return pl.pallas_call(
        kern,
        grid=(m // tm, n // tn, nk),
        in_specs=in_specs,
        out_specs=pl.BlockSpec((tm, tn), lambda i, j, k: (i, j)),
        out_shape=jax.ShapeDtypeStruct((m, n), out_dtype),
        compiler_params=_params(("parallel", "parallel", "arbitrary")),
        name=name,
    )(*args)


def _split3(x):
    hi = x.astype(BF16)
    r = x - hi.astype(F32)
    mid = r.astype(BF16)
    lo = (r - mid.astype(F32)).astype(BF16)
    return hi, mid, lo


def _cumsum_rows(tri, x):
    hi, mid, lo = _split3(x)
    d = lambda y: jnp.dot(tri, y, preferred_element_type=F32)
    return d(hi) + d(mid) + d(lo)


def _softplus(x):
    return jnp.maximum(x, 0.0) + jnp.log1p(jnp.exp(-jnp.abs(x)))


def _log_1p_exp_neg_abs(x):
    return jnp.log(1.0 + jnp.exp(-jnp.abs(x)))


def _log_sigmoid(x):
    return jnp.minimum(x, 0.0) - _log_1p_exp_neg_abs(x)


def _logaddexp(a, c):
    return jnp.maximum(a, c) + _log_1p_exp_neg_abs(a - c)


def _sigmoid(x):
    return 1.0 / (1.0 + jnp.exp(-x))


def _silu(x):
    h = 0.5 * x
    return h + h * jnp.tanh(h)


def _hgrn2_kernel(q_ref, f_ref, i_ref, g_ref, lbl_ref, on_ref, sums_ref, qmask_ref, levels_ref,
                  o_ref, st_ref, *, layer, nchunks):
    C, DK = HG_CHUNK, HG_DK

    lg = lbl_ref[...].astype(F32)
    e = jnp.exp(lg - jnp.max(lg, axis=0, keepdims=True))
    p = e / jnp.sum(e, axis=0, keepdims=True)
    lb = jnp.sum(p[:layer + 1], axis=0, keepdims=True) - p[0:1]
    log_lb = jnp.log(lb)
    log_1m_lb = jnp.log1p(-lb)
    out_w = on_ref[...].astype(F32)

    levels = levels_ref[...]
    heads = [slice(hh * DK, (hh + 1) * DK) for hh in range(HG_HEADS_PER_STEP)]

    @pl.when(pl.program_id(2) == 0)
    def _():
        st_ref[...] = jnp.zeros_like(st_ref)

    def chunk(c, carry):
        r0 = pl.multiple_of(c * C, C)
        qr = q_ref[0, pl.ds(r0, C), :].astype(F32)
        fr = f_ref[0, pl.ds(r0, C), :].astype(F32)
        v = i_ref[0, pl.ds(r0, C), :].astype(BF16)
        gr = g_ref[0, pl.ds(r0, C), :].astype(F32)

        q = _silu(qr)
        log_f = _logaddexp(log_lb, log_1m_lb + _log_sigmoid(fr))
        kk = (1.0 - lb) * _sigmoid(-fr)

        expo = jnp.dot(sums_ref[...], jnp.concatenate(_split3(log_f), axis=0), preferred_element_type=F32)
        dec = jnp.exp(expo)

        qb = q.astype(BF16)
        kb = kk.astype(BF16)
        a = [jnp.where(levels == 0, _dot_nt(qb[:, hd], kb[:, hd]), 0.0) for hd in heads]
        for lv in range(1, HG_NLEV + 1):
            isq = qmask_ref[(lv - 1) * C:lv * C, :] != 0.0
            z = (jnp.where(isq, q, kk) * dec[(lv - 1) * C:lv * C]).astype(BF16)
            a = [jnp.where(levels == lv, _dot_nt(z[:, hd], z[:, hd]), ah) for hd, ah in zip(heads, a)]

        qe = (q * dec[HG_NLEV * C:(HG_NLEV + 1) * C]).astype(BF16)
        k_dec = (kk * dec[(HG_NLEV + 1) * C:(HG_NLEV + 2) * C]).astype(BF16)
        e_last = dec[(HG_NLEV + 1) * C - 1:(HG_NLEV + 1) * C]
        outs = []
        for hh, hd in enumerate(heads):
            st = st_ref[hh]
            o = (jnp.dot(a[hh].astype(BF16), v[:, hd], preferred_element_type=F32)
                 + _dot_nt(qe[:, hd], st.astype(BF16)))
            st_ref[hh] = st * e_last[:, hd] + _dot_tn(v[:, hd], k_dec[:, hd])
            outs.append(o * lax.rsqrt(jnp.mean(o * o, axis=-1, keepdims=True) + EPS))
        o = jnp.concatenate(outs, axis=1) * out_w * _silu(gr)
        o_ref[0, pl.ds(r0, C), :] = o.astype(o_ref.dtype)
        return carry

    lax.fori_loop(0, nchunks, chunk, 0, unroll=8)


def _hgrn2_tables():
    C = HG_CHUNK
    r = np.arange(C)[:, None]
    c = np.arange(C)[None, :]
    blocks, qmasks = [], []
    levels = np.full((C, C), -1, np.int32)
    levels[r == c] = 0
    for lv in range(1, HG_NLEV + 1):
        m = C >> lv
        ref = (r // (2 * m)) * (2 * m) + m - 1
        isq = (r & m) != 0
        blocks.append(np.where(isq, (c > ref) & (c <= r), (c > r) & (c <= ref)))
        qmasks.append(np.broadcast_to(isq, (C, HG_HEADS_PER_STEP * HG_DK)))
        levels[(r > c) & (((r ^ c) >> (HG_NLEV - lv)) == 1)] = lv
    blocks.append(c <= r)
    blocks.append(c > r)
    sums = np.concatenate(blocks, axis=0).astype(np.float32)
    return (jnp.asarray(np.tile(sums, (1, 3)), BF16), jnp.asarray(np.concatenate(qmasks, 0), F32),
            jnp.asarray(levels))


def hgrn2_branch(q3, f3, ig3, lb_logits, out_norm, layer):
    bsz, s, _ = q3.shape
    w = HG_HEADS_PER_STEP * HG_DK
    ts = min(HG_SEQ_TILE, s)
    g_blk0 = HG_WIDTH // w
    blk = lambda off: pl.BlockSpec((1, ts, w), lambda b, h, t: (b, t, h + off))
    const = lambda a: pl.BlockSpec(a.shape, lambda b, h, t: (0, 0))
    sums, qmask, levels = _hgrn2_tables()
    kern = functools.partial(_hgrn2_kernel, layer=layer, nchunks=ts // HG_CHUNK)
    return pl.pallas_call(
        kern,
        grid=(bsz, HG_HEADS // HG_HEADS_PER_STEP, s // ts),
        in_specs=[blk(0), blk(0), blk(0), blk(g_blk0),
                  pl.BlockSpec((DEPTH, w), lambda b, h, t: (0, h)),
                  pl.BlockSpec((1, w), lambda b, h, t: (0, h)),
                  const(sums), const(qmask), const(levels)],
        out_specs=blk(0),
        out_shape=jax.ShapeDtypeStruct((bsz, s, HG_WIDTH), BF16),
        scratch_shapes=[pltpu.VMEM((HG_HEADS_PER_STEP, HG_DK, HG_DK), F32)],
        compiler_params=_params(("parallel", "parallel", "arbitrary")),
        name="hgrn2",
    )(q3, f3, ig3, ig3, lb_logits, out_norm.reshape(1, HG_WIDTH), sums, qmask, levels)


def _ssd_kernel(z_ref, xbc_ref, dt_ref, cw_ref, cb_ref, dtb_ref, alog_ref, dsk_ref, nw_ref,
                o_ref, tail_ref, xc_ref, st_ref, dxs_ref):
    C, G, N, GW = SSD_CHUNK, SSD_GROUPS, SSD_STATE, SSD_GW
    HALF = LANES // 2
    assert SSD_HEADDIM == HALF and C == LANES

    @pl.when(pl.program_id(1) == 0)
    def _():
        tail_ref[...] = jnp.zeros_like(tail_ref)
        st_ref[...] = jnp.zeros_like(st_ref)

    row = lax.broadcasted_iota(jnp.int32, (C, C), 0)
    col = lax.broadcasted_iota(jnp.int32, (C, C), 1)
    causal = row >= col
    tri = causal.astype(BF16)
    lane_lo = col < HALF
    row_lo = row < HALF

    dt = _softplus(dt_ref[0].astype(F32) + dtb_ref[...].astype(F32))
    a_neg = -jnp.exp(alog_ref[...].astype(F32))
    cs = _cumsum_rows(tri, dt * a_neg)
    cs_t = cs.T
    cs_last = cs[C - 1:C, :]
    dsk = dsk_ref[...].astype(F32)

    sr = lax.broadcasted_iota(jnp.int32, (SSD_CONV * C, C), 0)
    sc = lax.broadcasted_iota(jnp.int32, (SSD_CONV * C, C), 1)
    shifts = (sc + sr // C == sr % C).astype(BF16)
    row8 = lax.broadcasted_iota(jnp.int32, (8, 512), 0)
    for c0 in range(0, SSD_CONV_CH, 512):
        taps = jnp.dot(shifts, xbc_ref[0, :, c0:c0 + 512], preferred_element_type=F32)
        tail = tail_ref[:, c0:c0 + 512]
        w = cw_ref[:, c0:c0 + 512].astype(F32)
        acc = taps[0:C] * w[SSD_CONV - 1:SSD_CONV] + cb_ref[:, c0:c0 + 512].astype(F32)
        for k in range(1, SSD_CONV):
            sk = taps[k * C:(k + 1) * C]
            top = jnp.where(row8 < k, pltpu.roll(tail, k, axis=0), sk[0:8])
            acc = acc + jnp.concatenate([top, sk[8:]], axis=0) * w[SSD_CONV - 1 - k:SSD_CONV - k]
        tail_ref[:, c0:c0 + 512] = taps[C - 8:C]
        xc_ref[:, c0:c0 + 512] = _silu(acc)

    def pair(col0, col1):
        return jnp.where(lane_lo[:col0.shape[0]], col0, col1)

    for g in range(G):
        bg = xc_ref[:, SSD_INNER + g * N:SSD_INNER + (g + 1) * N].astype(BF16)
        cg = xc_ref[:, SSD_INNER + (G + g) * N:SSD_INNER + (G + g + 1) * N].astype(BF16)
        cb = _dot_nt(cg, bg)
        st = st_ref[g]
        y_state = _dot_nt(cg, st.astype(BF16))
        ys = []
        for pr in range(SSD_HPG // 2):
            h0 = g * SSD_HPG + 2 * pr
            h1 = h0 + 1
            c0 = g * GW + pr * LANES
            xp = xc_ref[:, c0:c0 + LANES]
            dxp = pair(dt[:, h0:h0 + 1], dt[:, h1:h1 + 1]) * xp
            w0 = cb * jnp.exp(jnp.where(causal, cs[:, h0:h0 + 1] - cs_t[h0:h0 + 1, :], NEG_INF))
            w1 = cb * jnp.exp(jnp.where(causal, cs[:, h1:h1 + 1] - cs_t[h1:h1 + 1, :], NEG_INF))
            y = (jnp.dot(w0.astype(BF16), jnp.where(lane_lo, dxp, 0.0).astype(BF16), preferred_element_type=F32)
                 + jnp.dot(w1.astype(BF16), jnp.where(lane_lo, 0.0, dxp).astype(BF16), preferred_element_type=F32))
            csp = pair(cs[:, h0:h0 + 1], cs[:, h1:h1 + 1])
            cs_last_p = pair(cs_last[:, h0:h0 + 1], cs_last[:, h1:h1 + 1])
            y = y + y_state[:, pr * LANES:(pr + 1) * LANES] * jnp.exp(csp)
            y = y + pair(dsk[:, h0:h0 + 1], dsk[:, h1:h1 + 1]) * xp
            ys.append(y)
            dxs_ref[:, pr * LANES:(pr + 1) * LANES] = (jnp.exp(cs_last_p - csp) * dxp).astype(BF16)
            e_last = jnp.exp(jnp.where(row_lo[:, 0:1], cs_last[:, h0:h0 + 1], cs_last[:, h1:h1 + 1]))
            st_ref[g, pr * LANES:(pr + 1) * LANES, :] = (
                e_last * st[pr * LANES:(pr + 1) * LANES, :]
                + _dot_tn(dxs_ref[:, pr * LANES:(pr + 1) * LANES], bg))
        yg = jnp.concatenate(ys, axis=1)
        yg = yg * _silu(z_ref[0, :, g * GW:(g + 1) * GW].astype(F32))
        yg = yg * lax.rsqrt(jnp.mean(yg * yg, axis=-1, keepdims=True) + EPS)
        o_ref[0, :, g * GW:(g + 1) * GW] = (yg * nw_ref[:, g * GW:(g + 1) * GW].astype(F32)).astype(o_ref.dtype)


def ssd_branch(zx3, z_blk, xbc_blk, dt3, conv_w, conv_b, dt_bias, a_log, d_skip, norm_w):
    bsz, s, _ = zx3.shape
    C = SSD_CHUNK
    assert zx3.dtype == BF16, "the conv row shifts are exact only for bf16 input"
    pad = lambda v: jnp.pad(v.astype(F32), (0, LANES - SSD_HEADS)).reshape(1, LANES)
    const = lambda shape: pl.BlockSpec(shape, lambda b, c: (0, 0))
    return pl.pallas_call(
        _ssd_kernel,
        grid=(bsz, s // C),
        in_specs=[pl.BlockSpec((1, C, SSD_INNER), lambda b, c: (b, c, z_blk)),
                  pl.BlockSpec((1, C, SSD_CONV_CH), lambda b, c: (b, c, xbc_blk)),
                  pl.BlockSpec((1, C, LANES), lambda b, c: (b, c, 0)),
                  const((SSD_CONV, SSD_CONV_CH)), const((1, SSD_CONV_CH)),
                  const((1, LANES)), const((1, LANES)), const((1, LANES)), const((1, SSD_INNER))],
        out_specs=pl.BlockSpec((1, C, SSD_INNER), lambda b, c: (b, c, 0)),
        out_shape=jax.ShapeDtypeStruct((bsz, s, SSD_INNER), BF16),
        scratch_shapes=[pltpu.VMEM((8, SSD_CONV_CH), F32),
                        pltpu.VMEM((C, SSD_CONV_CH), F32),
                        pltpu.VMEM((SSD_GROUPS, SSD_GW, SSD_STATE), F32),
                        pltpu.VMEM((C, SSD_GW), BF16)],
        compiler_params=_params(("parallel", "arbitrary")),
        name="ssd",
    )(zx3, zx3, dt3, conv_w, conv_b.reshape(1, SSD_CONV_CH), pad(dt_bias), pad(a_log), pad(d_skip),
      norm_w.reshape(1, SSD_INNER))


def _xattn_kernel(q_ref, k_ref, v_ref, o_ref):
    for hh in range(XA_HEADS_PER_STEP):
        hd = slice(hh * XA_HEAD_DIM, (hh + 1) * XA_HEAD_DIM)
        s = _dot_nt(q_ref[0, :, hd], k_ref[0, :, hd]) * (XA_HEAD_DIM ** -0.5)
        e = jnp.exp(s - jnp.max(s, axis=-1, keepdims=True))
        p = e / jnp.sum(e, axis=-1, keepdims=True)
        o_ref[0, :, hd] = jnp.dot(p.astype(BF16), v_ref[0, :, hd], preferred_element_type=F32).astype(o_ref.dtype)


def cross_attention(q3, k3, v3, ts=1024):
    bsz, s, _ = q3.shape
    m = k3.shape[1]
    ts = min(ts, s)
    w = XA_HEADS_PER_STEP * XA_HEAD_DIM
    return pl.pallas_call(
        _xattn_kernel,
        grid=(bsz, s // ts, XA_HEADS // XA_HEADS_PER_STEP),
        in_specs=[pl.BlockSpec((1, ts, w), lambda b, i, h: (b, i, h)),
                  pl.BlockSpec((1, m, w), lambda b, i, h: (b, 0, h)),
                  pl.BlockSpec((1, m, w), lambda b, i, h: (b, 0, h))],
        out_specs=pl.BlockSpec((1, ts, w), lambda b, i, h: (b, i, h)),
        out_shape=jax.ShapeDtypeStruct((bsz, s, D_MODEL), BF16),
        compiler_params=_params(("parallel", "parallel", "parallel")),
        name="xattn",
    )(q3, k3, v3)


def _mixer(h, res, bsz, s, layer, w_in, w_gate, lb_logits, hg_out_norm, conv_w, conv_b, dt_bias, a_log, d_skip,
           ssd_norm, gate_b, w_branch_a, w_branch_b, w_out):
    c_zx = 2 * HG_WIDTH
    c_dt = 4 * HG_WIDTH + SSD_INNER + SSD_CONV_CH
    proj = functools.partial(matmul, h, layer=layer, w_is_nk=True)
    q = proj(w_in, n=HG_WIDTH, w_col0=0, out_dtype=BF16, name="proj_q")
    f = proj(w_in, n=HG_WIDTH, w_col0=HG_WIDTH, out_dtype=F32, name="proj_f")
    igzx = proj(w_in, n=c_dt - c_zx, w_col0=c_zx, out_dtype=BF16, name="proj_igzx")
    dt = proj(w_in, n=LANES, w_col0=c_dt, out_dtype=F32, name="proj_dt")
    gates = proj(w_gate, n=2 * D_MODEL, out_dtype=BF16, name="proj_gates")

    r3 = lambda t: t.reshape(bsz, s, t.shape[-1])
    y_a = hgrn2_branch(r3(q), r3(f), r3(igzx), lb_logits, hg_out_norm, layer)
    y_b = ssd_branch(r3(igzx), (2 * HG_WIDTH) // SSD_INNER, (2 * HG_WIDTH + SSD_INNER) // SSD_CONV_CH,
                     r3(dt), conv_w, conv_b, dt_bias, a_log, d_skip, ssd_norm)
    m = bsz * s
    t = matmul(y_a.reshape(m, HG_WIDTH), w_branch_a, layer=layer, n=D_MODEL, out_dtype=F32,
               gate=(gates, 0, gate_b[:D_MODEL]), name="branch_a")
    merged = matmul(y_b.reshape(m, SSD_INNER), w_branch_b, layer=layer, n=D_MODEL, out_dtype=BF16,
                    gate=(gates, D_MODEL, gate_b[D_MODEL:]), add=t, name="branch_b")
    return matmul(merged, w_out, layer=layer, n=D_MODEL, out_dtype=F32, add=res, name="mix_out")


def kernel(x, mem, norm_mix, w_in, hg_lb_logits, hg_out_norm, ssd_conv_w, ssd_conv_b, ssd_dt_bias, ssd_a_log,
           ssd_d, ssd_norm, gate_b, w_branch_a, w_branch_b, w_out, norm_xattn, mem_norm, xa_wq, xa_wk, xa_wv,
           xa_wo, norm_mlp, mlp_w1, mlp_w2, final_norm):
    bsz, s, d = x.shape
    m = bsz * s
    mm = bsz * mem.shape[1]
    mem_n = rmsnorm(mem.reshape(mm, d), mem_norm, BF16)
    w_in_t = jnp.swapaxes(w_in, 1, 2)
    w_gate = w_in_t[:, IN_COLS - 2 * D_MODEL:, :].astype(BF16)
    w2 = mlp_w2.astype(BF16)
    res = x.reshape(m, d)
    for l in range(DEPTH):
        h = rmsnorm(res, norm_mix[l], BF16)
        res = _mixer(h, res, bsz, s, l, w_in_t, w_gate, hg_lb_logits, hg_out_norm[l], ssd_conv_w[l], ssd_conv_b[l],
                     ssd_dt_bias[l], ssd_a_log[l], ssd_d[l], ssd_norm[l], gate_b[l], w_branch_a, w_branch_b, w_out)

        h = rmsnorm(res, norm_xattn[l], BF16)
        q = matmul(h, xa_wq, layer=l, n=d, out_dtype=BF16, name="xa_q")
        k = matmul(mem_n, xa_wk, layer=l, n=d, out_dtype=BF16, name="xa_k")
        v = matmul(mem_n, xa_wv, layer=l, n=d, out_dtype=BF16, name="xa_v")
        o = cross_attention(q.reshape(bsz, s, d), k.reshape(bsz, -1, d), v.reshape(bsz, -1, d))
        res = matmul(o.reshape(m, d), xa_wo, layer=l, n=d, out_dtype=F32, add=res, name="xa_o")

        h = rmsnorm(res, norm_mlp[l], BF16)
        u = matmul(h, mlp_w1, layer=l, n=D_FF, out_dtype=BF16, relu2=True, name="mlp_up")
        res = matmul(u, w2, layer=l, n=d, out_dtype=F32, add=res, name="mlp_down")
    return rmsnorm(res, final_norm, x.dtype).reshape(bsz, s, d)
```
